```python
import math
import jax
import jax.numpy as jnp
from jax import lax
import numpy as np

D_MODEL = 1024
BATCH = 8
SEQ = 2048
DEPTH = 4
DEC_BATCH = 128
DEC_SEQ = 8
PAST_LEN = 16384
PAGE_SIZE = 128

D_FF = 2816
EPS = 1e-6
CHUNK = 128
A_GROUPS = 4
A_WIDTH = 512
A_GW = A_WIDTH // A_GROUPS
POOL_WINDOWS = (2, 4, 8, 16)
B_GROUPS = 4
B_WIDTH = 512
B_GW = B_WIDTH // B_GROUPS
POOL_BUF = 15
C_HEADS = 4
C_DQK = 128
C_DV = 256
C_QK_WIDTH = C_HEADS * C_DQK
C_V_WIDTH = C_HEADS * C_DV
MLSTM_CHUNK = 128
N_BRANCH = 3
IN_SPLITS = (A_WIDTH, A_WIDTH, B_WIDTH, C_QK_WIDTH, C_QK_WIDTH, C_V_WIDTH, C_V_WIDTH, C_HEADS, C_HEADS, N_BRANCH * D_MODEL)
IN_COLS = 7688

kernel_name = 'hybrid_gmlp_pool_mlstm_decoder_step'


def rmsnorm(x, g):
    xf = x.astype(jnp.float32)
    y = xf * lax.rsqrt(jnp.mean(xf * xf, axis=-1, keepdims=True) + EPS)
    return (y * g.astype(jnp.float32)).astype(x.dtype)


def layernorm(x, g, b):
    xf = x.astype(jnp.float32)
    mu = jnp.mean(xf, axis=-1, keepdims=True)
    xc = xf - mu
    var = jnp.mean(xc * xc, axis=-1, keepdims=True)
    y = xc * lax.rsqrt(var + EPS) * g.astype(jnp.float32) + b.astype(jnp.float32)
    return y.astype(x.dtype)


def swiglu(x, w_in, w_down):
    gate, up = jnp.split(x @ w_in, 2, axis=-1)
    return (jax.nn.silu(gate) * up) @ w_down


def chunk_mlp_mix(v, w_s, b_s):
    bsz, L, _ = v.shape
    if L <= CHUNK:
        T, n_chunks = L, 1
    else:
        T, n_chunks = CHUNK, -(-L // CHUNK)
    Lp = T * n_chunks
    ws = jnp.tril(w_s[:, :T, :T])
    vp = jnp.pad(v, ((0, 0), (0, Lp - L), (0, 0))).reshape(bsz, n_chunks, T, A_GROUPS, A_GW)
    bias = jnp.swapaxes(b_s[:, :T], 0, 1)[None, None, :, :, None]
    s = jnp.einsum('gts,bcsgd->bctgd', ws, vp) + bias
    return s.reshape(bsz, Lp, A_WIDTH)[:, :L]


def pool_mix(xb, prev, pool_w, pool_scale):
    bsz, L, _ = xb.shape
    if prev is None:
        xcat, n_prev, start = xb, 0, 0
    else:
        xcat = jnp.concatenate([prev.astype(xb.dtype), xb], axis=1)
        n_prev = prev.shape[1]
        start = PAST_LEN - n_prev
    Lc = xcat.shape[1]
    xf = xcat.astype(jnp.float32)
    pos = start + jnp.arange(Lc, dtype=jnp.int32)
    outs = []
    for g, w in enumerate(POOL_WINDOWS):
        xg = xf[..., g * B_GW:(g + 1) * B_GW]
        cs = jnp.cumsum(xg, axis=1)
        lower = jnp.pad(cs[:, :Lc - w], ((0, 0), (w, 0), (0, 0)))
        cnt = jnp.minimum(pos + 1, w).astype(jnp.float32)[None, :, None]
        outs.append((cs - lower) / cnt - xg)
    pooled = jnp.stack(outs, axis=2)[:, n_prev:]
    mixed = jnp.einsum('blgc,gcd->blgd', pooled, pool_w.astype(jnp.float32)).reshape(bsz, L, B_WIDTH)
    out = (mixed * pool_scale.astype(jnp.float32)).astype(xb.dtype)
    return out, xcat[:, Lc - POOL_BUF:]


def mlstm(q, k, v, ig, lf, c0, n0, m0):
    bsz, L = q.shape[0], q.shape[1]
    T = MLSTM_CHUNK if L % MLSTM_CHUNK == 0 else math.gcd(L, MLSTM_CHUNK)
    nc = L // T

    def to_chunks(a):
        return jnp.swapaxes(a.reshape((bsz, nc, T) + a.shape[2:]), 0, 1)

    causal = jnp.tril(jnp.ones((T, T), dtype=bool))

    def step(carry, xs):
        C, n, m = carry
        qc, kc, vc, ic, fc = xs
        b = jnp.cumsum(fc, axis=1)
        a = ic - b
        m_t = b + jnp.maximum(m[:, None], lax.cummax(a, axis=1))
        inter = jnp.exp(b + m[:, None] - m_t)
        bh = jnp.swapaxes(b - m_t, 1, 2)
        ah = jnp.swapaxes(a, 1, 2)
        log_d = bh[..., :, None] + ah[..., None, :]
        d = jnp.exp(jnp.where(causal, log_d, -jnp.inf))
        w = jnp.einsum('bthk,bshk->bhts', qc, kc) * d
        num = jnp.einsum('bhts,bshv->bthv', w, vc) + inter[..., None] * jnp.einsum('bhvk,bthk->bthv', C, qc)
        den = jnp.swapaxes(jnp.sum(w, axis=-1), 1, 2) + inter * jnp.einsum('bhk,bthk->bth', n, qc)
        h = num / jnp.maximum(jnp.abs(den), jnp.exp(-m_t))[..., None]
        m_end = m_t[:, -1]
        decay = jnp.exp(b[:, -1] + m - m_end)
        w_in = jnp.exp(a + (b[:, -1] - m_end)[:, None])
        C_new = decay[..., None, None] * C + jnp.einsum('bsh,bshv,bshk->bhvk', w_in, vc, kc)
        n_new = decay[..., None] * n + jnp.einsum('bsh,bshk->bhk', w_in, kc)
        return (C_new, n_new, m_end), h

    carry0 = (c0.astype(jnp.float32), n0.astype(jnp.float32), m0.astype(jnp.float32))
    xs = (to_chunks(q), to_chunks(k), to_chunks(v), to_chunks(ig), to_chunks(lf))
    (c_f, n_f, m_f), hs = lax.scan(step, carry0, xs)
    h = jnp.swapaxes(hs, 0, 1).reshape(bsz, L, C_HEADS, C_DV)
    return h, c_f, n_f, m_f


def trunk_layer(x, lp, pool_prev, c0, n0, m0):
    bsz, L, _ = x.shape
    f32 = jnp.float32
    x = x + 0.5 * rmsnorm(swiglu(rmsnorm(x, lp['ffn1_norm_pre']), lp['ffn1_w_in'], lp['ffn1_w_down']), lp['ffn1_norm_post'])
    h = rmsnorm(x, lp['mix_norm_pre'])
    z = h @ lp['w_in']
    split_at = [int(i) for i in np.cumsum(IN_SPLITS)[:-1]]
    a_u, a_v, b_in, c_q, c_k, c_v, c_o, c_i, c_f, g_all = jnp.split(z, split_at, axis=-1)
    v_n = layernorm(jax.nn.gelu(a_v), lp['gmlp_ln_g'], lp['gmlp_ln_b'])
    out_a = jax.nn.gelu(a_u) * chunk_mlp_mix(v_n, lp['gmlp_w_s'], lp['gmlp_b_s'])
    out_b, pool_buf = pool_mix(b_in, pool_prev, lp['pool_w'], lp['pool_scale'])
    q = c_q.reshape(bsz, L, C_HEADS, C_DQK).astype(f32)
    k = c_k.reshape(bsz, L, C_HEADS, C_DQK).astype(f32) * (C_DQK ** -0.5)
    v = c_v.reshape(bsz, L, C_HEADS, C_DV).astype(f32)
    ig = c_i.astype(f32) + lp['mlstm_i_bias'].astype(f32)
    lf = jax.nn.log_sigmoid(c_f.astype(f32) + lp['mlstm_f_bias'].astype(f32))
    hc, c_new, n_new, m_new = mlstm(q, k, v, ig, lf, c0, n0, m0)
    hc = hc * lax.rsqrt(jnp.mean(hc * hc, axis=-1, keepdims=True) + EPS) * lp['mlstm_norm_g'].astype(f32).reshape(C_HEADS, C_DV)
    out_c = jax.nn.sigmoid(c_o) * hc.reshape(bsz, L, C_V_WIDTH).astype(x.dtype)
    g_a, g_b, g_c = jnp.split(jax.nn.sigmoid(g_all), N_BRANCH, axis=-1)
    merged = g_a * (out_a @ lp['proj_a']) + g_b * (out_b @ lp['proj_b']) + g_c * (out_c @ lp['proj_c'])
    x = x + rmsnorm(merged @ lp['w_out'], lp['mix_norm_post'])
    x = x + 0.5 * rmsnorm(swiglu(rmsnorm(x, lp['ffn2_norm_pre']), lp['ffn2_w_in'], lp['ffn2_w_down']), lp['ffn2_norm_post'])
    return x, v_n, pool_buf, c_new, n_new, m_new


def setup_inputs(seed: int = 0) -> dict:
    key = jax.random.key(seed)
    keys = jax.random.split(key, 32)
    ks = [keys[i] for i in range(32)]

    def nrm(shape, scale):
        return jax.random.normal(ks.pop(), shape, jnp.float32) * scale

    def gain(shape):
        return 1.0 + nrm(shape, 0.05)

    inp = {}
    inp['x_prompt'] = nrm((BATCH, SEQ, D_MODEL), 1.0)
    inp['x_sample'] = nrm((DEC_BATCH, DEC_SEQ, D_MODEL), 1.0)
    inp['state_pool'] = nrm((DEPTH, DEC_BATCH, POOL_BUF, B_WIDTH), 1.0)
    inp['state_mlstm_C'] = nrm((DEPTH, DEC_BATCH, C_HEADS, C_DV, C_DQK), 0.1)
    inp['state_mlstm_n'] = nrm((DEPTH, DEC_BATCH, C_HEADS, C_DQK), 0.3)
    inp['state_mlstm_m'] = nrm((DEPTH, DEC_BATCH, C_HEADS), 0.5)
    inp['ffn1_norm_pre'] = gain((DEPTH, D_MODEL))
    inp['ffn1_w_in'] = nrm((DEPTH, D_MODEL, 2 * D_FF), D_MODEL ** -0.5)
    inp['ffn1_w_down'] = nrm((DEPTH, D_FF, D_MODEL), D_FF ** -0.5)
    inp['ffn1_norm_post'] = gain((DEPTH, D_MODEL))
    inp['mix_norm_pre'] = gain((DEPTH, D_MODEL))
    inp['w_in'] = nrm((DEPTH, D_MODEL, IN_COLS), D_MODEL ** -0.5)
    inp['gmlp_ln_g'] = gain((DEPTH, A_WIDTH))
    inp['gmlp_ln_b'] = nrm((DEPTH, A_WIDTH), 0.02)
    inp['gmlp_w_s'] = nrm((DEPTH, A_GROUPS, CHUNK, CHUNK), CHUNK ** -0.5)
    inp['gmlp_b_s'] = 1.0 + nrm((DEPTH, A_GROUPS, CHUNK), 0.1)
    inp['pool_w'] = nrm((DEPTH, B_GROUPS, B_GW, B_GW), B_GW ** -0.5)
    inp['pool_scale'] = gain((DEPTH, B_WIDTH))
    inp['mlstm_i_bias'] = nrm((DEPTH, C_HEADS), 0.1)
    inp['mlstm_f_bias'] = 3.0 + nrm((DEPTH, C_HEADS), 0.5)
    inp['mlstm_norm_g'] = gain((DEPTH, C_V_WIDTH))
    inp['proj_a'] = nrm((DEPTH, A_WIDTH, D_MODEL), A_WIDTH ** -0.5)
    inp['proj_b'] = nrm((DEPTH, B_WIDTH, D_MODEL), B_WIDTH ** -0.5)
    inp['proj_c'] = nrm((DEPTH, C_V_WIDTH, D_MODEL), C_V_WIDTH ** -0.5)
    inp['w_out'] = nrm((DEPTH, D_MODEL, D_MODEL), D_MODEL ** -0.5)
    inp['mix_norm_post'] = gain((DEPTH, D_MODEL))
    inp['ffn2_norm_pre'] = gain((DEPTH, D_MODEL))
    inp['ffn2_w_in'] = nrm((DEPTH, D_MODEL, 2 * D_FF), D_MODEL ** -0.5)
    inp['ffn2_w_down'] = nrm((DEPTH, D_FF, D_MODEL), D_FF ** -0.5)
    inp['ffn2_norm_post'] = gain((DEPTH, D_MODEL))
    return inp


def reference(x_prompt, x_sample, state_pool, state_mlstm_C, state_mlstm_n, state_mlstm_m,
              ffn1_norm_pre, ffn1_w_in, ffn1_w_down, ffn1_norm_post,
              mix_norm_pre, w_in, gmlp_ln_g, gmlp_ln_b, gmlp_w_s, gmlp_b_s,
              pool_w, pool_scale, mlstm_i_bias, mlstm_f_bias, mlstm_norm_g,
              proj_a, proj_b, proj_c, w_out, mix_norm_post,
              ffn2_norm_pre, ffn2_w_in, ffn2_w_down, ffn2_norm_post):
    yp, ys = x_prompt, x_sample
    bp = x_prompt.shape[0]
    pool_p, c_p, n_p, m_p = [], [], [], []
    pool_s, c_s, n_s, m_s, v_s = [], [], [], [], []
    for l in range(DEPTH):
        lp = dict(ffn1_norm_pre=ffn1_norm_pre[l], ffn1_w_in=ffn1_w_in[l], ffn1_w_down=ffn1_w_down[l],
                  ffn1_norm_post=ffn1_norm_post[l], mix_norm_pre=mix_norm_pre[l], w_in=w_in[l],
                  gmlp_ln_g=gmlp_ln_g[l], gmlp_ln_b=gmlp_ln_b[l], gmlp_w_s=gmlp_w_s[l], gmlp_b_s=gmlp_b_s[l],
                  pool_w=pool_w[l], pool_scale=pool_scale[l], mlstm_i_bias=mlstm_i_bias[l],
                  mlstm_f_bias=mlstm_f_bias[l], mlstm_norm_g=mlstm_norm_g[l], proj_a=proj_a[l],
                  proj_b=proj_b[l], proj_c=proj_c[l], w_out=w_out[l], mix_norm_post=mix_norm_post[l],
                  ffn2_norm_pre=ffn2_norm_pre[l], ffn2_w_in=ffn2_w_in[l], ffn2_w_down=ffn2_w_down[l],
                  ffn2_norm_post=ffn2_norm_post[l])
        c0 = jnp.zeros((bp, C_HEADS, C_DV, C_DQK), jnp.float32)
        n0 = jnp.zeros((bp, C_HEADS, C_DQK), jnp.float32)
        m0 = jnp.zeros((bp, C_HEADS), jnp.float32)
        yp, _, pb, cf, nf, mf = trunk_layer(yp, lp, None, c0, n0, m0)
        pool_p.append(pb); c_p.append(cf); n_p.append(nf); m_p.append(mf)
        ys, vn, pbs, cfs, nfs, mfs = trunk_layer(ys, lp, state_pool[l], state_mlstm_C[l], state_mlstm_n[l], state_mlstm_m[l])
        pool_s.append(pbs); c_s.append(cfs); n_s.append(nfs); m_s.append(mfs); v_s.append(vn)
    return (yp, ys,
            jnp.stack(pool_p), jnp.stack(c_p), jnp.stack(n_p), jnp.stack(m_p),
            jnp.stack(pool_s), jnp.stack(c_s), jnp.stack(n_s), jnp.stack(m_s), jnp.stack(v_s))
```

```python
import functools

import jax
import jax.numpy as jnp
from jax import lax
from jax.experimental import pallas as pl
from jax.experimental.pallas import tpu as pltpu

F32 = jnp.float32
BF16 = jnp.bfloat16

D_MODEL = 1024
D_FF = 2816
DEPTH = 4
EPS = 1e-6
PAST_LEN = 16384
N_GROUPS = 4
GROUP_W = 128
POOL_WINDOWS = (2, 4, 8, 16)
POOL_HIST = 16
HEADS = 4
DQK = 128
DV = 256
BLOCK_ROWS = 128
SAMPLE_SEQS_PER_BLOCK = 16
SAMPLE_STEPS = 8

OFF_AU, OFF_AV, OFF_B, OFF_Q, OFF_K, OFF_V, OFF_O, OFF_G = 0, 512, 1024, 1536, 2048, 2560, 3584, 4608
WCAT_COLS = OFF_G + 3 * D_MODEL

FFN_TM = 1024
FFN_TF = 256
PROMPT_SEQS_PER_STEP = 4
VMEM_LIMIT_BYTES = 56 * 1024 * 1024


def _dot(a, b):
    return jnp.dot(a, b, preferred_element_type=F32)


def _dot_nt(a, b):
    return lax.dot_general(a, b, (((1,), (1,)), ((), ())), preferred_element_type=F32)


def _dot_tn(a, b):
    return lax.dot_general(a, b, (((0,), (0,)), ((), ())), preferred_element_type=F32)


def _rms(x, g):
    return x * lax.rsqrt(jnp.mean(x * x, axis=-1, keepdims=True) + EPS) * g


def _sigmoid(x):
    return 1.0 / (1.0 + jnp.exp(-x))


def _gelu(x):
    return 0.5 * x * (1.0 + jnp.tanh(0.7978845608028654 * (x + 0.044715 * (x * x * x))))


def _log_sigmoid(x):
    return jnp.minimum(x, 0.0) - jnp.log1p(jnp.exp(-jnp.abs(x)))


def _const_spec(shape, index, ngrid):
    zeros = (0,) * (len(shape) - 1)
    if ngrid == 1:
        imap = lambda i: (index,) + zeros
    else:
        imap = lambda i, j: (index,) + zeros
    return pl.BlockSpec((None,) + tuple(shape[1:]), imap, pipeline_mode=pl.Buffered(1))


def _ffn_kernel(x_ref, gpre_ref, win_ref, wdown_ref, gpost_ref, o_ref, h_scr):
    f = pl.program_id(1)

    @pl.when(f == 0)
    def _():
        h_scr[...] = _rms(x_ref[...], gpre_ref[...]).astype(BF16)

    z = _dot(h_scr[...], win_ref[...])
    gate = z[:, :FFN_TF]
    up = z[:, FFN_TF:]
    act = (gate * _sigmoid(gate) * up).astype(BF16)
    part = _dot(act, wdown_ref[...])

    @pl.when(f == 0)
    def _():
        o_ref[...] = part

    @pl.when(f > 0)
    def _():
        o_ref[...] += part

    @pl.when(f == pl.num_programs(1) - 1)
    def _():
        o_ref[...] = x_ref[...] + 0.5 * _rms(o_ref[...], gpost_ref[...])


def _ffn(x2d, gpre, win_r, wdown, gpost, layer):
    m = x2d.shape[0]
    tm = min(FFN_TM, m)
    nf = D_FF // FFN_TF
    return pl.pallas_call(
        _ffn_kernel,
        grid=(m // tm, nf),
        in_specs=[
            pl.BlockSpec((tm, D_MODEL), lambda i, f: (i, 0)),
            pl.BlockSpec((None, 1, D_MODEL), lambda i, f: (layer, 0, 0)),
            pl.BlockSpec((None, D_MODEL, 2 * FFN_TF), lambda i, f: (layer, 0, f)),
            pl.BlockSpec((None, FFN_TF, D_MODEL), lambda i, f: (layer, f, 0)),
            pl.BlockSpec((None, 1, D_MODEL), lambda i, f: (layer, 0, 0)),
        ],
        out_specs=pl.BlockSpec((tm, D_MODEL), lambda i, f: (i, 0)),
        out_shape=jax.ShapeDtypeStruct((m, D_MODEL), F32),
        scratch_shapes=[pltpu.VMEM((tm, D_MODEL), BF16)],
        compiler_params=pltpu.CompilerParams(
            dimension_semantics=("parallel", "arbitrary"), vmem_limit_bytes=VMEM_LIMIT_BYTES),
        name="ffn",
    )(x2d, gpre, win_r, wdown, gpost)


def _branch_a(h, wcat_ref, lng_ref, lnb_ref, mixm_ref, mixb_ref, outa_scr, vn_out_ref=None):
    rows = h.shape[0]
    a_v = _gelu(_dot(h, wcat_ref[:, OFF_AV:OFF_AV + 512]))
    mu = jnp.mean(a_v, axis=-1, keepdims=True)
    xc = a_v - mu
    var = jnp.mean(xc * xc, axis=-1, keepdims=True)
    v_n = xc * lax.rsqrt(var + EPS) * lng_ref[...] + lnb_ref[...]
    if vn_out_ref is not None:
        vn_out_ref[...] = v_n
    vb = v_n.astype(BF16)
    a_u = _gelu(_dot(h, wcat_ref[:, OFF_AU:OFF_AU + 512]))
    for g in range(N_GROUPS):
        cols = slice(g * GROUP_W, (g + 1) * GROUP_W)
        mix_g = mixm_ref[g]
        bias_g = mixb_ref[g]
        for s in range(rows // BLOCK_ROWS):
            rws = slice(s * BLOCK_ROWS, (s + 1) * BLOCK_ROWS)
            mixed = _dot(mix_g, vb[rws, cols]) + bias_g
            outa_scr[rws, cols] = (a_u[rws, cols] * mixed).astype(BF16)


def _branch_b(h, wcat_ref, poolw_ref, pscale_ref, hist_in, hist_out, pooled_scr, outb_scr, *, rs, pos1):
    rows = h.shape[0]
    hist_rows = POOL_HIST * rs
    b_in = _dot(h, wcat_ref[:, OFF_B:OFF_B + 512])
    for s in range(rows // BLOCK_ROWS):
        rws = slice(s * BLOCK_ROWS, (s + 1) * BLOCK_ROWS)
        cur = b_in[rws, :]
        ext = jnp.concatenate([hist_in(s), cur], axis=0)
        for g, w in enumerate(POOL_WINDOWS):
            cols = slice(g * GROUP_W, (g + 1) * GROUP_W)
            acc = ext[:, cols]
            sh = 1
            while sh < w:
                acc = acc + pltpu.roll(acc, sh * rs, axis=0)
                sh *= 2
            win = acc[hist_rows:, :]
            if pos1 is None:
                mean = win * (1.0 / w)
            else:
                mean = win / jnp.minimum(pos1, w).astype(F32)
            pooled_scr[rws, cols] = (mean - cur[:, cols]).astype(BF16)
        hist_out(s, ext[BLOCK_ROWS:, :])
    for g in range(N_GROUPS):
        cols = slice(g * GROUP_W, (g + 1) * GROUP_W)
        mixed = _dot(pooled_scr[:, cols], poolw_ref[g]) * pscale_ref[:, cols]
        outb_scr[:, cols] = mixed.astype(BF16)


def _merge(x, h, hc, wcat_ref, pa_ref, pb_ref, pc_ref, wout_ref, gpost_ref, outa_scr, outb_scr):
    out_c = (_sigmoid(_dot(h, wcat_ref[:, OFF_O:OFF_O + D_MODEL])) * hc).astype(BF16)
    merged = _sigmoid(_dot(h, wcat_ref[:, OFF_G:OFF_G + D_MODEL])) * _dot(outa_scr[...], pa_ref[...])
    merged += _sigmoid(_dot(h, wcat_ref[:, OFF_G + D_MODEL:OFF_G + 2 * D_MODEL])) * _dot(outb_scr[...], pb_ref[...])
    merged += _sigmoid(_dot(h, wcat_ref[:, OFF_G + 2 * D_MODEL:OFF_G + 3 * D_MODEL])) * _dot(out_c, pc_ref[...])
    y = _dot(merged.astype(BF16), wout_ref[...])
    return x + _rms(y, gpost_ref[...])


def _scan_lanes(x, op, fill):
    lane = lax.broadcasted_iota(jnp.int32, x.shape, 1)
    sh = 1
    while sh < x.shape[1]:
        x = op(x, jnp.where(lane >= sh, pltpu.roll(x, sh, axis=1), fill))
        sh *= 2
    return x


def _last_lane(x):
    lane = lax.broadcasted_iota(jnp.int32, x.shape, 1)
    return jnp.max(jnp.where(lane == x.shape[1] - 1, x, -jnp.inf), axis=1, keepdims=True)


def _mlstm_prompt(h, wcat_ref, wift_ref, gbias_ref, normg_ref, c_ref, n_ref, m_ref, hc_scr, *, nseq):
    t_len = BLOCK_ROWS
    q = _dot(h, wcat_ref[:, OFF_Q:OFF_Q + 512])
    k = _dot(h, wcat_ref[:, OFF_K:OFF_K + 512]) * (DQK ** -0.5)
    v = _dot(h, wcat_ref[:, OFF_V:OFF_V + D_MODEL])
    grow = _dot_nt(wift_ref[...], h)
    ri = lax.broadcasted_iota(jnp.int32, (t_len, t_len), 0)
    ci = lax.broadcasted_iota(jnp.int32, (t_len, t_len), 1)
    causal = ci <= ri
    for s in range(nseq):
        rws = slice(s * t_len, (s + 1) * t_len)
        ig = grow[0:16, rws] + gbias_ref[0:16, :]
        lf = _log_sigmoid(grow[16:32, rws] + gbias_ref[16:32, :])
        b = _scan_lanes(lf, jnp.add, 0.0)
        a = ig - b
        m0 = m_ref[s]
        m_t = b + jnp.maximum(m0, _scan_lanes(a, jnp.maximum, -jnp.inf))
        inter = jnp.exp(b + m0 - m_t)
        b_last = _last_lane(b)
        m_end = _last_lane(m_t)
        decay = jnp.exp(b_last + m0 - m_end)
        w_in = jnp.exp(a + b_last - m_end)
        zt = jnp.concatenate([b - m_t, inter, jnp.exp(-m_t), w_in, jnp.zeros((64, t_len), F32)], axis=0).T
        m_ref[s] = jnp.broadcast_to(m_end, (16, t_len))
        for hd in range(HEADS):
            qh = q[rws, hd * DQK:(hd + 1) * DQK]
            kh = k[rws, hd * DQK:(hd + 1) * DQK]
            vh = v[rws, hd * DV:(hd + 1) * DV]
            qb = qh.astype(BF16)
            kb = kh.astype(BF16)
            bm_c = zt[:, hd:hd + 1]
            inter_c = zt[:, 16 + hd:17 + hd]
            en_c = zt[:, 32 + hd:33 + hd]
            win_c = zt[:, 48 + hd:49 + hd]
            dmat = jnp.exp(jnp.where(causal, bm_c + a[hd:hd + 1, :], -jnp.inf))
            wmat = _dot_nt(qb, kb) * dmat
            c_old = c_ref[s, hd]
            n_old = n_ref[s, hd:hd + 1, :]
            num = _dot(wmat.astype(BF16), vh.astype(BF16)) + inter_c * _dot_nt(qb, c_old.astype(BF16))
            den = jnp.sum(wmat, axis=1, keepdims=True) + inter_c * jnp.sum(qh * n_old, axis=1, keepdims=True)
            hout = num / jnp.maximum(jnp.abs(den), en_c)
            hc_scr[rws, hd * DV:(hd + 1) * DV] = _rms(hout, normg_ref[:, hd * DV:(hd + 1) * DV])
            dec_row = decay[hd:hd + 1, :]
            c_ref[s, hd] = dec_row * c_old + _dot_tn((vh * win_c).astype(BF16), kb)
            n_ref[s, hd:hd + 1, :] = dec_row * n_old + jnp.sum(kh * win_c, axis=0, keepdims=True)


def _mix_prompt_kernel(x_ref, gpre_ref, wcat_ref, wift_ref, gbias_ref, lng_ref, lnb_ref, mixm_ref, mixb_ref,
                       poolw_ref, pscale_ref, normg_ref, pa_ref, pb_ref, pc_ref, wout_ref, gpost_ref,
                       o_ref, pool_ref, c_ref, n_ref, m_ref,
                       outa_scr, outb_scr, pooled_scr, hc_scr, *, nseq):
    c = pl.program_id(1)
    rows = nseq * BLOCK_ROWS

    @pl.when(c == 0)
    def _():
        pool_ref[...] = jnp.zeros(pool_ref.shape, F32)
        c_ref[...] = jnp.zeros(c_ref.shape, F32)
        n_ref[...] = jnp.zeros(n_ref.shape, F32)
        m_ref[...] = jnp.zeros(m_ref.shape, F32)

    x = x_ref[...].reshape(rows, D_MODEL)
    h = _rms(x, gpre_ref[...]).astype(BF16)

    _branch_a(h, wcat_ref, lng_ref, lnb_ref, mixm_ref, mixb_ref, outa_scr)

    pos1 = c * BLOCK_ROWS + 1 + lax.broadcasted_iota(jnp.int32, (BLOCK_ROWS, GROUP_W), 0)

    def hist_out(s, val):
        pool_ref[s] = val

    _branch_b(h, wcat_ref, poolw_ref, pscale_ref, lambda s: pool_ref[s], hist_out, pooled_scr, outb_scr,
              rs=1, pos1=pos1)

    _mlstm_prompt(h, wcat_ref, wift_ref, gbias_ref, normg_ref, c_ref, n_ref, m_ref, hc_scr, nseq=nseq)

    y = _merge(x, h, hc_scr[...], wcat_ref, pa_ref, pb_ref, pc_ref, wout_ref, gpost_ref, outa_scr, outb_scr)
    o_ref[...] = y.reshape(nseq, BLOCK_ROWS, D_MODEL)


def _mix_prompt(x3d, lw, layer):
    nb, seq, _ = x3d.shape
    nseq = PROMPT_SEQS_PER_STEP
    rows = nseq * BLOCK_ROWS
    cs = functools.partial(_const_spec, index=layer, ngrid=2)
    weights = [lw['mix_norm_pre'], lw['wcat'], lw['wift'], lw['gbias'], lw['ln_g'], lw['ln_b'],
               lw['mixm_p'], lw['mixb_p'], lw['pool_w'], lw['pool_scale'], lw['norm_g'],
               lw['proj_a'], lw['proj_b'], lw['proj_c'], lw['w_out'], lw['mix_norm_post']]
    out_shapes = (
        jax.ShapeDtypeStruct((nb, seq, D_MODEL), F32),
        jax.ShapeDtypeStruct((nb, POOL_HIST, 512), F32),
        jax.ShapeDtypeStruct((nb, HEADS, DV, DQK), F32),
        jax.ShapeDtypeStruct((nb, 8, DQK), F32),
        jax.ShapeDtypeStruct((nb, 16, BLOCK_ROWS), F32),
    )
    out_specs = (
        pl.BlockSpec((nseq, BLOCK_ROWS, D_MODEL), lambda g, c: (g, c, 0)),
        pl.BlockSpec((nseq, POOL_HIST, 512), lambda g, c: (g, 0, 0)),
        pl.BlockSpec((nseq, HEADS, DV, DQK), lambda g, c: (g, 0, 0, 0)),
        pl.BlockSpec((nseq, 8, DQK), lambda g, c: (g, 0, 0)),
        pl.BlockSpec((nseq, 16, BLOCK_ROWS), lambda g, c: (g, 0, 0)),
    )
    return pl.pallas_call(
        functools.partial(_mix_prompt_kernel, nseq=nseq),
        grid=(nb // nseq, seq // BLOCK_ROWS),
        in_specs=[pl.BlockSpec((nseq, BLOCK_ROWS, D_MODEL), lambda g, c: (g, c, 0))]
        + [cs(w.shape) for w in weights],
        out_specs=out_specs,
        out_shape=out_shapes,
        scratch_shapes=[pltpu.VMEM((rows, 512), BF16), pltpu.VMEM((rows, 512), BF16),
                        pltpu.VMEM((rows, 512), BF16), pltpu.VMEM((rows, D_MODEL), F32)],
        compiler_params=pltpu.CompilerParams(
            dimension_semantics=("parallel", "arbitrary"), vmem_limit_bytes=VMEM_LIMIT_BYTES),
        name="mix_prompt",
    )(x3d, *weights)


def _mix_s1_kernel(x_ref, hist_ref, gpre_ref, wcat_ref, wifc_ref, lng_ref, lnb_ref, mixm_ref, mixb_ref,
                   poolw_ref, pscale_ref,
                   outa_ref, outb_ref, q_ref, k_ref, v_ref, gi_ref, gf_ref, vn_ref, pool_ref,
                   pooled_scr):
    h = _rms(x_ref[...], gpre_ref[...]).astype(BF16)
    _branch_a(h, wcat_ref, lng_ref, lnb_ref, mixm_ref, mixb_ref, outa_ref, vn_out_ref=vn_ref)

    def hist_out(s, val):
        pool_ref[s] = val

    _branch_b(h, wcat_ref, poolw_ref, pscale_ref, lambda s: hist_ref[s], hist_out, pooled_scr, outb_ref,
              rs=SAMPLE_SEQS_PER_BLOCK, pos1=None)
    q = _dot(h, wcat_ref[:, OFF_Q:OFF_Q + 512])
    k = _dot(h, wcat_ref[:, OFF_K:OFF_K + 512]) * (DQK ** -0.5)
    v = _dot(h, wcat_ref[:, OFF_V:OFF_V + D_MODEL])
    for j in range(512 // 128):
        q_ref[j] = q[:, j * 128:(j + 1) * 128]
        k_ref[j] = k[:, j * 128:(j + 1) * 128]
    for j in range(D_MODEL // 128):
        v_ref[j] = v[:, j * 128:(j + 1) * 128]
    gi_ref[...] = _dot(h, wifc_ref[:, 0:128])
    gf_ref[...] = _dot(h, wifc_ref[:, 128:256])


def _mix_s2_kernel(q_ref, k_ref, v_ref, gi_ref, gf_ref, ib_ref, fb_ref, normg_ref, c_ref, n_ref, m_ref,
                   hc_ref, cn_ref, nn_ref, mn_ref,
                   cq_scr, win_scr, dec_scr):
    nb = SAMPLE_SEQS_PER_BLOCK
    steps = SAMPLE_STEPS

    vblocks = DV // 128

    def slab(ref, t):
        return ref[t * nb:(t + 1) * nb, :]

    def slab_v(ref, hd, t):
        return jnp.concatenate([ref[hd * vblocks + j, t * nb:(t + 1) * nb, :] for j in range(vblocks)], axis=1)

    m0 = m_ref[...]
    ig = [slab(gi_ref, t) + ib_ref[...] for t in range(steps)]
    lf = [_log_sigmoid(slab(gf_ref, t) + fb_ref[...]) for t in range(steps)]
    b, a, m_t = [], [], []
    run_b = jnp.zeros_like(m0)
    run_max = jnp.full(m0.shape, -jnp.inf, F32)
    for t in range(steps):
        run_b = run_b + lf[t]
        b.append(run_b)
        a.append(ig[t] - run_b)
        run_max = jnp.maximum(run_max, a[t])
        m_t.append(run_b + jnp.maximum(m0, run_max))
    inter = [jnp.exp(b[t] + m0 - m_t[t]) for t in range(steps)]
    m_end = m_t[-1]
    decay = jnp.exp(b[-1] + m0 - m_end)
    w_in = [jnp.exp(a[t] + b[-1] - m_end) for t in range(steps)]
    mn_ref[...] = m_end
    for t in range(steps):
        win_scr[t * nb:(t + 1) * nb, :] = w_in[t]
        dec_scr[t * nb:(t + 1) * nb, :] = decay

    def seq_body(bi, carry):
        rows = pl.ds(bi, steps, stride=nb)
        wcols = win_scr[rows, :]
        dcols = dec_scr[rows, :]
        for hd in range(HEADS):
            c_old = c_ref[bi, hd]
            qb = q_ref[hd, rows, :].astype(BF16)
            cq = _dot_nt(qb, c_old.astype(BF16))
            for j in range(vblocks):
                cq_scr[hd * vblocks + j, rows, :] = cq[:, j * 128:(j + 1) * 128]
            vrows = jnp.concatenate([v_ref[hd * vblocks + j, rows, :] for j in range(vblocks)], axis=1)
            vs = (vrows * wcols[:, hd:hd + 1]).astype(BF16)
            kb = k_ref[hd, rows, :].astype(BF16)
            cn_ref[bi, hd] = dcols[0:1, hd:hd + 1] * c_old + _dot_tn(vs, kb)
        return carry

    lax.fori_loop(0, nb, seq_body, 0)

    for hd in range(HEADS):
        qcols = slice(hd * DQK, (hd + 1) * DQK)
        vcols = slice(hd * DV, (hd + 1) * DV)
        col = slice(hd, hd + 1)
        qs = [slab(q_ref.at[hd], t) for t in range(steps)]
        ks = [slab(k_ref.at[hd], t) for t in range(steps)]
        vs = [slab_v(v_ref, hd, t) for t in range(steps)]
        n_old = n_ref[:, qcols]
        n_new = decay[:, col] * n_old
        for t in range(steps):
            n_new = n_new + w_in[t][:, col] * ks[t]
        nn_ref[:, qcols] = n_new
        for t in range(steps):
            num = inter[t][:, col] * slab_v(cq_scr, hd, t)
            den = inter[t][:, col] * jnp.sum(qs[t] * n_old, axis=1, keepdims=True)
            bm = b[t][:, col] - m_t[t][:, col]
            for s in range(t + 1):
                wts = jnp.sum(qs[t] * ks[s], axis=1, keepdims=True) * jnp.exp(bm + a[s][:, col])
                num = num + wts * vs[s]
                den = den + wts
            hout = num / jnp.maximum(jnp.abs(den), jnp.exp(-m_t[t][:, col]))
            hc_ref[t * nb:(t + 1) * nb, vcols] = _rms(hout, normg_ref[:, vcols])


def _mix_s3_kernel(x_ref, outa_ref, outb_ref, hc_ref, gpre_ref, wcat_ref, pa_ref, pb_ref, pc_ref, wout_ref,
                   gpost_ref, o_ref):
    x = x_ref[...]
    h = _rms(x, gpre_ref[...]).astype(BF16)
    o_ref[...] = _merge(x, h, hc_ref[...], wcat_ref, pa_ref, pb_ref, pc_ref, wout_ref, gpost_ref,
                        outa_ref, outb_ref)


def _mix_sample(xs, hist, c_in, n_in, m_in, lw, layer):
    rows = xs.shape[0]
    nblk = rows // BLOCK_ROWS
    hist_rows = POOL_HIST * SAMPLE_SEQS_PER_BLOCK
    cs1 = functools.partial(_const_spec, index=layer, ngrid=1)
    full = lambda shape: pl.BlockSpec(shape, lambda i: (0,) * len(shape))
    params = pltpu.CompilerParams(dimension_semantics=("arbitrary",), vmem_limit_bytes=VMEM_LIMIT_BYTES)

    w1 = [lw['mix_norm_pre'], lw['wcat'], lw['wifc'], lw['ln_g'], lw['ln_b'], lw['mixm_s'], lw['mixb_s'],
          lw['pool_w'], lw['pool_scale']]
    s1_shapes = (
        jax.ShapeDtypeStruct((rows, 512), BF16),
        jax.ShapeDtypeStruct((rows, 512), BF16),
        jax.ShapeDtypeStruct((HEADS, rows, DQK), F32),
        jax.ShapeDtypeStruct((HEADS, rows, DQK), F32),
        jax.ShapeDtypeStruct((D_MODEL // 128, rows, 128), F32),
        jax.ShapeDtypeStruct((rows, 128), F32),
        jax.ShapeDtypeStruct((rows, 128), F32),
        jax.ShapeDtypeStruct((rows, 512), F32),
        jax.ShapeDtypeStruct((nblk, hist_rows, 512), F32),
    )
    outa, outb, q, k, v, gi, gf, vn, pool_new = pl.pallas_call(
        _mix_s1_kernel,
        grid=(1,),
        in_specs=[full(xs.shape), full(hist.shape)] + [cs1(w.shape) for w in w1],
        out_specs=tuple(full(s.shape) for s in s1_shapes),
        out_shape=s1_shapes,
        scratch_shapes=[pltpu.VMEM((rows, 512), BF16)],
        compiler_params=params,
        name="mix_s1",
    )(xs, hist, *w1)

    nb = SAMPLE_SEQS_PER_BLOCK
    blk = lambda width: pl.BlockSpec((BLOCK_ROWS, width), lambda j: (j, 0))
    seqblk = lambda width: pl.BlockSpec((nb, width), lambda j: (j, 0))
    cblk = pl.BlockSpec((nb, HEADS, DV, DQK), lambda j: (j, 0, 0, 0))
    lrow = lambda width: pl.BlockSpec((None, 1, width), lambda j: (layer, 0, 0))
    colblk = lambda n: pl.BlockSpec((n, BLOCK_ROWS, 128), lambda j: (0, j, 0))
    hc, c_new, n_new, m_new = pl.pallas_call(
        _mix_s2_kernel,
        grid=(nblk,),
        in_specs=[colblk(HEADS), colblk(HEADS), colblk(D_MODEL // 128), blk(128), blk(128),
                  lrow(128), lrow(128), lrow(D_MODEL), cblk, seqblk(512), seqblk(128)],
        out_specs=(blk(D_MODEL), cblk, seqblk(512), seqblk(128)),
        out_shape=(jax.ShapeDtypeStruct((rows, D_MODEL), F32),
                   jax.ShapeDtypeStruct(c_in.shape, F32),
                   jax.ShapeDtypeStruct(n_in.shape, F32),
                   jax.ShapeDtypeStruct(m_in.shape, F32)),
        scratch_shapes=[pltpu.VMEM((D_MODEL // 128, BLOCK_ROWS, 128), F32), pltpu.VMEM((BLOCK_ROWS, 128), F32),
                        pltpu.VMEM((BLOCK_ROWS, 128), F32)],
        compiler_params=pltpu.CompilerParams(dimension_semantics=("parallel",),
                                             vmem_limit_bytes=VMEM_LIMIT_BYTES),
        name="mix_s2",
    )(q, k, v, gi, gf, lw['ib'], lw['fb'], lw['norm_g'], c_in, n_in, m_in)

    w3 = [lw['mix_norm_pre'], lw['wcat'], lw['proj_a'], lw['proj_b'], lw['proj_c'], lw['w_out'],
          lw['mix_norm_post']]
    y = pl.pallas_call(
        _mix_s3_kernel,
        grid=(1,),
        in_specs=[full(xs.shape), full(outa.shape), full(outb.shape), full(hc.shape)]
        + [cs1(w.shape) for w in w3],
        out_specs=full(xs.shape),
        out_shape=jax.ShapeDtypeStruct(xs.shape, F32),
        compiler_params=params,
        name="mix_s3",
    )(xs, outa, outb, hc, *w3)
    return y, vn, pool_new, c_new, n_new, m_new


def _to_blocked(x):
    nseq, t, c = x.shape
    nb = SAMPLE_SEQS_PER_BLOCK
    return x.reshape(nseq // nb, nb, t, c).transpose(0, 2, 1, 3).reshape(nseq * t, c)


def _from_blocked(x, t):
    rows, c = x.shape
    nb = SAMPLE_SEQS_PER_BLOCK
    nseq = rows // t
    return x.reshape(nseq // nb, t, nb, c).transpose(0, 2, 1, 3).reshape(nseq, t, c)


def _prep_weights(p):
    w_in = p['w_in']
    nf = D_FF // FFN_TF

    def ffn_in(w):
        return w.reshape(DEPTH, D_MODEL, 2, nf, FFN_TF).transpose(0, 1, 3, 2, 4).reshape(
            DEPTH, D_MODEL, 2 * D_FF).astype(BF16)

    row = lambda a: a.reshape(DEPTH, 1, -1).astype(F32)
    wi = w_in[:, :, 4608:4612]
    wf = w_in[:, :, 4612:4616]
    wift = jnp.zeros((DEPTH, 32, D_MODEL), F32)
    wift = wift.at[:, 0:HEADS].set(wi.transpose(0, 2, 1)).at[:, 16:16 + HEADS].set(wf.transpose(0, 2, 1))
    wifc = jnp.zeros((DEPTH, D_MODEL, 256), F32)
    wifc = wifc.at[:, :, 0:HEADS].set(wi).at[:, :, 128:128 + HEADS].set(wf)
    gbias = jnp.zeros((DEPTH, 32, BLOCK_ROWS), F32)
    gbias = gbias.at[:, 0:HEADS].set(jnp.broadcast_to(p['mlstm_i_bias'][:, :, None], (DEPTH, HEADS, BLOCK_ROWS)))
    gbias = gbias.at[:, 16:16 + HEADS].set(
        jnp.broadcast_to(p['mlstm_f_bias'][:, :, None], (DEPTH, HEADS, BLOCK_ROWS)))
    lane_pad = lambda a: jnp.zeros((DEPTH, 1, 128), F32).at[:, 0, 0:HEADS].set(a)

    ws = p['gmlp_w_s']
    bs = p['gmlp_b_s']
    mixm_p = jnp.tril(ws)
    mixb_p = jnp.broadcast_to(bs[..., None], ws.shape)
    nb = SAMPLE_SEQS_PER_BLOCK
    tril_s = jnp.tril(ws[:, :, :SAMPLE_STEPS, :SAMPLE_STEPS])
    eye = jnp.eye(nb, dtype=F32)
    mixm_s = jnp.einsum('lgts,bc->lgtbsc', tril_s, eye).reshape(DEPTH, N_GROUPS, BLOCK_ROWS, BLOCK_ROWS)
    mixb_s = jnp.broadcast_to(jnp.repeat(bs[:, :, :SAMPLE_STEPS], nb, axis=2)[..., None], ws.shape)

    return dict(
        ffn1_norm_pre=row(p['ffn1_norm_pre']), ffn1_w_in=ffn_in(p['ffn1_w_in']),
        ffn1_w_down=p['ffn1_w_down'].astype(BF16), ffn1_norm_post=row(p['ffn1_norm_post']),
        ffn2_norm_pre=row(p['ffn2_norm_pre']), ffn2_w_in=ffn_in(p['ffn2_w_in']),
        ffn2_w_down=p['ffn2_w_down'].astype(BF16), ffn2_norm_post=row(p['ffn2_norm_post']),
        mix_norm_pre=row(p['mix_norm_pre']), mix_norm_post=row(p['mix_norm_post']),
        wcat=jnp.concatenate([w_in[:, :, :4608], w_in[:, :, 4616:]], axis=2).astype(BF16),
        wift=wift.astype(BF16), wifc=wifc.astype(BF16), gbias=gbias,
        ib=lane_pad(p['mlstm_i_bias']), fb=lane_pad(p['mlstm_f_bias']),
        ln_g=row(p['gmlp_ln_g']), ln_b=row(p['gmlp_ln_b']),
        mixm_p=mixm_p.astype(BF16), mixb_p=mixb_p.astype(F32),
        mixm_s=mixm_s.astype(BF16), mixb_s=mixb_s.astype(F32),
        pool_w=p['pool_w'].astype(BF16), pool_scale=row(p['pool_scale']), norm_g=row(p['mlstm_norm_g']),
        proj_a=p['proj_a'].astype(BF16), proj_b=p['proj_b'].astype(BF16), proj_c=p['proj_c'].astype(BF16),
        w_out=p['w_out'].astype(BF16),
    )


def kernel(x_prompt, x_sample, state_pool, state_mlstm_C, state_mlstm_n, state_mlstm_m, ffn1_norm_pre, ffn1_w_in, ffn1_w_down, ffn1_norm_post, mix_norm_pre, w_in, gmlp_ln_g, gmlp_ln_b, gmlp_w_s, gmlp_b_s, pool_w, pool_scale, mlstm_i_bias, mlstm_f_bias, mlstm_norm_g, proj_a, proj_b, proj_c, w_out, mix_norm_post, ffn2_norm_pre, ffn2_w_in, ffn2_w_down, ffn2_norm_post):
    params = dict(ffn1_norm_pre=ffn1_norm_pre, ffn1_w_in=ffn1_w_in, ffn1_w_down=ffn1_w_down,
                  ffn1_norm_post=ffn1_norm_post, mix_norm_pre=mix_norm_pre, w_in=w_in, gmlp_ln_g=gmlp_ln_g,
                  gmlp_ln_b=gmlp_ln_b, gmlp_w_s=gmlp_w_s, gmlp_b_s=gmlp_b_s, pool_w=pool_w,
                  pool_scale=pool_scale, mlstm_i_bias=mlstm_i_bias, mlstm_f_bias=mlstm_f_bias,
                  mlstm_norm_g=mlstm_norm_g, proj_a=proj_a, proj_b=proj_b, proj_c=proj_c, w_out=w_out,
                  mix_norm_post=mix_norm_post, ffn2_norm_pre=ffn2_norm_pre, ffn2_w_in=ffn2_w_in,
                  ffn2_w_down=ffn2_w_down, ffn2_norm_post=ffn2_norm_post)
    lw = _prep_weights(params)
    nbp, seq, _ = x_prompt.shape
    nbs, steps, _ = x_sample.shape

    yp = x_prompt
    ys = _to_blocked(x_sample)
    hist_all = jnp.pad(state_pool, ((0, 0), (0, 0), (1, 0), (0, 0)))
    n_all = state_mlstm_n.reshape(DEPTH, nbs, HEADS * DQK)
    m_all = jnp.pad(state_mlstm_m, ((0, 0), (0, 0), (0, 128 - HEADS)))

    pool_p, c_p, n_p, m_p = [], [], [], []
    pool_s, c_s, n_s, m_s, v_s = [], [], [], [], []
    for l in range(DEPTH):
        ffn1 = (lw['ffn1_norm_pre'], lw['ffn1_w_in'], lw['ffn1_w_down'], lw['ffn1_norm_post'])
        ffn2 = (lw['ffn2_norm_pre'], lw['ffn2_w_in'], lw['ffn2_w_down'], lw['ffn2_norm_post'])
        yp = _ffn(yp.reshape(nbp * seq, D_MODEL), *ffn1, l).reshape(nbp, seq, D_MODEL)
        yp, pb, cf, nf_, mf = _mix_prompt(yp, lw, l)
        yp = _ffn(yp.reshape(nbp * seq, D_MODEL), *ffn2, l).reshape(nbp, seq, D_MODEL)
        pool_p.append(pb[:, 1:, :])
        c_p.append(cf)
        n_p.append(nf_[:, :HEADS, :])
        m_p.append(mf[:, :HEADS, 0])
        ys = _ffn(ys, *ffn1, l)
        hist = _to_blocked(hist_all[l]).reshape(nbs // SAMPLE_SEQS_PER_BLOCK,
                                                POOL_HIST * SAMPLE_SEQS_PER_BLOCK, 512)
        ys, vn, pool_new, c_new, n_new, m_new = _mix_sample(ys, hist, state_mlstm_C[l], n_all[l], m_all[l], lw, l)
        ys = _ffn(ys, *ffn2, l)
        pool_s.append(_from_blocked(pool_new.reshape(nbs * POOL_HIST, 512), POOL_HIST)[:, 1:, :])
        c_s.append(c_new)
        n_s.append(n_new.reshape(nbs, HEADS, DQK))
        m_s.append(m_new[:, :HEADS])
        v_s.append(_from_blocked(vn, steps))
    return (yp, _from_blocked(ys, steps),
            jnp.stack(pool_p), jnp.stack(c_p), jnp.stack(n_p), jnp.stack(m_p),
            jnp.stack(pool_s), jnp.stack(c_s), jnp.stack(n_s), jnp.stack(m_s), jnp.stack(v_s))
```

```python
import functools

import jax
import jax.numpy as jnp
from jax import lax
from jax.experimental import pallas as pl
from jax.experimental.pallas import tpu as pltpu

F32 = jnp.float32
BF16 = jnp.bfloat16

D_MODEL = 1024
D_FF = 2816
DEPTH = 4
EPS = 1e-6
PAST_LEN = 16384
N_GROUPS = 4
GROUP_W = 128
POOL_WINDOWS = (2, 4, 8, 16)
POOL_HIST = 16
HEADS = 4
DQK = 128
DV = 256
BLOCK_ROWS = 128
SAMPLE_SEQS_PER_BLOCK = 16
SAMPLE_STEPS = 8

OFF_AU, OFF_AV, OFF_B, OFF_Q, OFF_K, OFF_V, OFF_O = 0, 512, 1024, 1536, 2048, 2560, 3584
WCAT_COLS = 4608
GATE_COLS = 8

FFN_TM = 1024
FFN_TF = 256
PROMPT_SEQS_PER_STEP = 4
VMEM_LIMIT_BYTES = 56 * 1024 * 1024


def _dot(a, b):
    return jnp.dot(a, b, preferred_element_type=F32)


def _dot_nt(a, b):
    return lax.dot_general(a, b, (((1,), (1,)), ((), ())), preferred_element_type=F32)


def _dot_tn(a, b):
    return lax.dot_general(a, b, (((0,), (0,)), ((), ())), preferred_element_type=F32)


def _rms(x, g):
    return x * lax.rsqrt(jnp.mean(x * x, axis=-1, keepdims=True) + EPS) * g


def _sigmoid(x):
    return 1.0 / (1.0 + jnp.exp(-x))


def _gelu(x):
    return 0.5 * x * (1.0 + jnp.tanh(0.7978845608028654 * (x + 0.044715 * (x * x * x))))


def _log_sigmoid(x):
    return jnp.minimum(x, 0.0) - jnp.log1p(jnp.exp(-jnp.abs(x)))


def _const_spec(shape, index, ngrid):
    zeros = (0,) * (len(shape) - 1)
    if ngrid == 1:
        imap = lambda i: (index,) + zeros
    else:
        imap = lambda i, j: (index,) + zeros
    return pl.BlockSpec((None,) + tuple(shape[1:]), imap, pipeline_mode=pl.Buffered(1))


def _ffn_kernel(x_ref, gpre_ref, wgate_ref, wup_ref, wdown_ref, gpost_ref, o_ref, h_scr):
    f = pl.program_id(1)

    @pl.when(f == 0)
    def _():
        h_scr[...] = _rms(x_ref[...], gpre_ref[...]).astype(BF16)
        o_ref[...] = jnp.zeros(o_ref.shape, F32)

    h = h_scr[...]
    gate = _dot(h, wgate_ref[...].astype(BF16))
    up = _dot(h, wup_ref[...].astype(BF16))
    act = (gate * _sigmoid(gate) * up).astype(BF16)
    o_ref[...] += _dot(act, wdown_ref[...].astype(BF16))

    @pl.when(f == pl.num_programs(1) - 1)
    def _():
        o_ref[...] = x_ref[...] + 0.5 * _rms(o_ref[...], gpost_ref[...])


def _ffn(x2d, gpre, w_in, wdown, gpost, layer):
    m = x2d.shape[0]
    tm = min(FFN_TM, m)
    nf = D_FF // FFN_TF
    return pl.pallas_call(
        _ffn_kernel,
        grid=(m // tm, nf),
        in_specs=[
            pl.BlockSpec((tm, D_MODEL), lambda i, f: (i, 0)),
            pl.BlockSpec((None, 1, D_MODEL), lambda i, f: (layer, 0, 0)),
            pl.BlockSpec((None, D_MODEL, FFN_TF), lambda i, f: (layer, 0, f)),
            pl.BlockSpec((None, D_MODEL, FFN_TF), lambda i, f: (layer, 0, f + nf)),
            pl.BlockSpec((None, FFN_TF, D_MODEL), lambda i, f: (layer, f, 0)),
            pl.BlockSpec((None, 1, D_MODEL), lambda i, f: (layer, 0, 0)),
        ],
        out_specs=pl.BlockSpec((tm, D_MODEL), lambda i, f: (i, 0)),
        out_shape=jax.ShapeDtypeStruct((m, D_MODEL), F32),
        scratch_shapes=[pltpu.VMEM((tm, D_MODEL), BF16)],
        compiler_params=pltpu.CompilerParams(
            dimension_semantics=("parallel", "arbitrary"), vmem_limit_bytes=VMEM_LIMIT_BYTES),
        name="ffn",
    )(x2d, gpre, w_in, w_in, wdown, gpost)


def _branch_a(h, wcat_ref, lng_ref, lnb_ref, mixm_ref, mixb_ref, outa_scr, vn_out_ref=None):
    rows = h.shape[0]
    a_v = _gelu(_dot(h, wcat_ref[:, OFF_AV:OFF_AV + 512]))
    mu = jnp.mean(a_v, axis=-1, keepdims=True)
    xc = a_v - mu
    var = jnp.mean(xc * xc, axis=-1, keepdims=True)
    v_n = xc * lax.rsqrt(var + EPS) * lng_ref[...] + lnb_ref[...]
    if vn_out_ref is not None:
        vn_out_ref[...] = v_n
    vb = v_n.astype(BF16)
    a_u = _gelu(_dot(h, wcat_ref[:, OFF_AU:OFF_AU + 512]))
    for g in range(N_GROUPS):
        cols = slice(g * GROUP_W, (g + 1) * GROUP_W)
        mix_g = mixm_ref[g]
        bias_g = mixb_ref[g]
        for s in range(rows // BLOCK_ROWS):
            rws = slice(s * BLOCK_ROWS, (s + 1) * BLOCK_ROWS)
            mixed = _dot(mix_g, vb[rws, cols]) + bias_g
            outa_scr[rws, cols] = (a_u[rws, cols] * mixed).astype(BF16)


def _branch_b(h, wcat_ref, poolw_ref, pscale_ref, hist_in, hist_out, pooled_scr, outb_scr, *, rs, pos1):
    rows = h.shape[0]
    hist_rows = POOL_HIST * rs
    b_in = _dot(h, wcat_ref[:, OFF_B:OFF_B + 512])
    for s in range(rows // BLOCK_ROWS):
        rws = slice(s * BLOCK_ROWS, (s + 1) * BLOCK_ROWS)
        cur = b_in[rws, :]
        ext = jnp.concatenate([hist_in(s), cur], axis=0)
        for g, w in enumerate(POOL_WINDOWS):
            cols = slice(g * GROUP_W, (g + 1) * GROUP_W)
            acc = ext[:, cols]
            sh = 1
            while sh < w:
                acc = acc + pltpu.roll(acc, sh * rs, axis=0)
                sh *= 2
            win = acc[hist_rows:, :]
            if pos1 is None:
                mean = win * (1.0 / w)
            else:
                mean = win / jnp.minimum(pos1, w).astype(F32)
            pooled_scr[rws, cols] = (mean - cur[:, cols]).astype(BF16)
        hist_out(s, ext[BLOCK_ROWS:, :])
    for g in range(N_GROUPS):
        cols = slice(g * GROUP_W, (g + 1) * GROUP_W)
        mixed = _dot(pooled_scr[:, cols], poolw_ref[g]) * pscale_ref[:, cols]
        outb_scr[:, cols] = mixed.astype(BF16)


def _merge(x, h, hc, wcat_ref, wg_ref, pa_ref, pb_ref, pc_ref, wout_ref, gpost_ref, outa_scr, outb_scr):
    merged = _sigmoid(_dot(h, wg_ref[:, 0:D_MODEL])) * _dot(outa_scr[...], pa_ref[...])
    merged += _sigmoid(_dot(h, wg_ref[:, D_MODEL:2 * D_MODEL])) * _dot(outb_scr[...], pb_ref[...])
    out_c = (_sigmoid(_dot(h, wcat_ref[:, OFF_O:OFF_O + D_MODEL])) * hc).astype(BF16)
    merged += _sigmoid(_dot(h, wg_ref[:, 2 * D_MODEL:3 * D_MODEL])) * _dot(out_c, pc_ref[...])
    y = _dot(merged.astype(BF16), wout_ref[...])
    return x + _rms(y, gpost_ref[...])


def _scan_lanes(x, op, fill):
    lane = lax.broadcasted_iota(jnp.int32, x.shape, 1)
    sh = 1
    while sh < x.shape[1]:
        x = op(x, jnp.where(lane >= sh, pltpu.roll(x, sh, axis=1), fill))
        sh *= 2
    return x


def _last_lane(x):
    lane = lax.broadcasted_iota(jnp.int32, x.shape, 1)
    return jnp.max(jnp.where(lane == x.shape[1] - 1, x, -jnp.inf), axis=1, keepdims=True)


def _mlstm_gates(h, wift_ref, gbias_ref, m_ref, *, nseq):
    t_len = BLOCK_ROWS
    grow = _dot_nt(wift_ref[...], h)
    seqs = range(nseq)
    ig = jnp.concatenate([grow[0:16, s * t_len:(s + 1) * t_len] + gbias_ref[0:16, :] for s in seqs], axis=0)
    lf = _log_sigmoid(
        jnp.concatenate([grow[16:32, s * t_len:(s + 1) * t_len] + gbias_ref[16:32, :] for s in seqs], axis=0))
    m0 = m_ref[...].reshape(nseq * 16, t_len)
    b = _scan_lanes(lf, jnp.add, 0.0)
    a = ig - b
    m_t = b + jnp.maximum(m0, _scan_lanes(a, jnp.maximum, -jnp.inf))
    inter = jnp.exp(b + m0 - m_t)
    b_last = _last_lane(b)
    m_end = _last_lane(m_t)
    decay = jnp.exp(b_last + m0 - m_end)
    w_in = jnp.exp(a + b_last - m_end)
    m_ref[...] = jnp.broadcast_to(m_end, (nseq * 16, t_len)).reshape(nseq, 16, t_len)
    stacked = jnp.concatenate([b - m_t, inter, jnp.exp(-m_t), w_in], axis=0)
    pad = (-stacked.shape[0]) % t_len
    if pad:
        stacked = jnp.concatenate([stacked, jnp.zeros((pad, t_len), F32)], axis=0)
    cols_t = [stacked[i * t_len:(i + 1) * t_len, :].T for i in range(stacked.shape[0] // t_len)]

    def col(quantity, s, hd):
        idx = quantity * nseq * 16 + s * 16 + hd
        return cols_t[idx // t_len][:, idx % t_len:idx % t_len + 1]

    return a, decay, col


def _mlstm_prompt(h, gates, wcat_ref, normg_ref, c_ref, n_ref, hc_scr, *, nseq):
    t_len = BLOCK_ROWS
    a_all, decay_all, col = gates
    q = _dot(h, wcat_ref[:, OFF_Q:OFF_Q + 512])
    k = _dot(h, wcat_ref[:, OFF_K:OFF_K + 512]) * (DQK ** -0.5)
    v = _dot(h, wcat_ref[:, OFF_V:OFF_V + D_MODEL])
    ri = lax.broadcasted_iota(jnp.int32, (t_len, t_len), 0)
    ci = lax.broadcasted_iota(jnp.int32, (t_len, t_len), 1)
    causal = ci <= ri
    for s in range(nseq):
        rws = slice(s * t_len, (s + 1) * t_len)
        a = a_all[s * 16:(s + 1) * 16, :]
        decay = decay_all[s * 16:(s + 1) * 16, :]
        for hd in range(HEADS):
            qh = q[rws, hd * DQK:(hd + 1) * DQK]
            kh = k[rws, hd * DQK:(hd + 1) * DQK]
            vh = v[rws, hd * DV:(hd + 1) * DV]
            qb = qh.astype(BF16)
            kb = kh.astype(BF16)
            bm_c = col(0, s, hd)
            inter_c = col(1, s, hd)
            en_c = col(2, s, hd)
            win_c = col(3, s, hd)
            dmat = jnp.exp(jnp.where(causal, bm_c + a[hd:hd + 1, :], -jnp.inf))
            wmat = _dot_nt(qb, kb) * dmat
            c_old = c_ref[s, hd]
            n_old = n_ref[s, hd:hd + 1, :]
            num = _dot(wmat.astype(BF16), vh.astype(BF16)) + inter_c * _dot_nt(qb, c_old.astype(BF16))
            den = jnp.sum(wmat, axis=1, keepdims=True) + inter_c * jnp.sum(qh * n_old, axis=1, keepdims=True)
            hout = num / jnp.maximum(jnp.abs(den), en_c)
            hc_scr[rws, hd * DV:(hd + 1) * DV] = _rms(hout, normg_ref[:, hd * DV:(hd + 1) * DV])
            dec_row = decay[hd:hd + 1, :]
            c_ref[s, hd] = dec_row * c_old + _dot_tn((vh * win_c).astype(BF16), kb)
            n_ref[s, hd:hd + 1, :] = dec_row * n_old + jnp.sum(kh * win_c, axis=0, keepdims=True)


def _mix_prompt_kernel(x_ref, gpre_ref, wcat_ref, wg_ref, wift_ref, gbias_ref, lng_ref, lnb_ref, mixm_ref, mixb_ref,
                       poolw_ref, pscale_ref, normg_ref, pa_ref, pb_ref, pc_ref, wout_ref, gpost_ref,
                       o_ref, pool_ref, c_ref, n_ref, m_ref,
                       outa_scr, outb_scr, pooled_scr, hc_scr, *, nseq):
    c = pl.program_id(1)
    rows = nseq * BLOCK_ROWS

    @pl.when(c == 0)
    def _():
        pool_ref[...] = jnp.zeros(pool_ref.shape, F32)
        c_ref[...] = jnp.zeros(c_ref.shape, F32)
        n_ref[...] = jnp.zeros(n_ref.shape, F32)
        m_ref[...] = jnp.zeros(m_ref.shape, F32)

    x = x_ref[...].reshape(rows, D_MODEL)
    h = _rms(x, gpre_ref[...]).astype(BF16)

    gates = _mlstm_gates(h, wift_ref, gbias_ref, m_ref, nseq=nseq)

    _branch_a(h, wcat_ref, lng_ref, lnb_ref, mixm_ref, mixb_ref, outa_scr)

    pos1 = c * BLOCK_ROWS + 1 + lax.broadcasted_iota(jnp.int32, (BLOCK_ROWS, GROUP_W), 0)

    def hist_out(s, val):
        pool_ref[s] = val

    _branch_b(h, wcat_ref, poolw_ref, pscale_ref, lambda s: pool_ref[s], hist_out, pooled_scr, outb_scr,
              rs=1, pos1=pos1)

    _mlstm_prompt(h, gates, wcat_ref, normg_ref, c_ref, n_ref, hc_scr, nseq=nseq)

    y = _merge(x, h, hc_scr[...], wcat_ref, wg_ref, pa_ref, pb_ref, pc_ref, wout_ref, gpost_ref,
               outa_scr, outb_scr)
    o_ref[...] = y.reshape(nseq, BLOCK_ROWS, D_MODEL)


def _mix_prompt(x3d, lw, layer):
    nb, seq, _ = x3d.shape
    nseq = PROMPT_SEQS_PER_STEP
    rows = nseq * BLOCK_ROWS
    cs = functools.partial(_const_spec, index=layer, ngrid=2)
    weights = [lw['mix_norm_pre'], lw['wcat'], lw['wg'], lw['wift'], lw['gbias'], lw['ln_g'], lw['ln_b'],
               lw['mixm_p'], lw['mixb_p'], lw['pool_w'], lw['pool_scale'], lw['norm_g'],
               lw['proj_a'], lw['proj_b'], lw['proj_c'], lw['w_out'], lw['mix_norm_post']]
    out_shapes = (
        jax.ShapeDtypeStruct((nb, seq, D_MODEL), F32),
        jax.ShapeDtypeStruct((nb, POOL_HIST, 512), F32),
        jax.ShapeDtypeStruct((nb, HEADS, DV, DQK), F32),
        jax.ShapeDtypeStruct((nb, 8, DQK), F32),
        jax.ShapeDtypeStruct((nb, 16, BLOCK_ROWS), F32),
    )
    out_specs = (
        pl.BlockSpec((nseq, BLOCK_ROWS, D_MODEL), lambda g, c: (g, c, 0)),
        pl.BlockSpec((nseq, POOL_HIST, 512), lambda g, c: (g, 0, 0)),
        pl.BlockSpec((nseq, HEADS, DV, DQK), lambda g, c: (g, 0, 0, 0)),
        pl.BlockSpec((nseq, 8, DQK), lambda g, c: (g, 0, 0)),
        pl.BlockSpec((nseq, 16, BLOCK_ROWS), lambda g, c: (g, 0, 0)),
    )
    return pl.pallas_call(
        functools.partial(_mix_prompt_kernel, nseq=nseq),
        grid=(nb // nseq, seq // BLOCK_ROWS),
        in_specs=[pl.BlockSpec((nseq, BLOCK_ROWS, D_MODEL), lambda g, c: (g, c, 0))]
        + [cs(w.shape) for w in weights],
        out_specs=out_specs,
        out_shape=out_shapes,
        scratch_shapes=[pltpu.VMEM((rows, 512), BF16), pltpu.VMEM((rows, 512), BF16),
                        pltpu.VMEM((rows, 512), BF16), pltpu.VMEM((rows, D_MODEL), F32)],
        compiler_params=pltpu.CompilerParams(
            dimension_semantics=("parallel", "arbitrary"), vmem_limit_bytes=VMEM_LIMIT_BYTES),
        name="mix_prompt",
    )(x3d, *weights)


def _mix_s1_kernel(x_ref, hist_ref, gpre_ref, wcat_ref, wifc_ref, lng_ref, lnb_ref, mixm_ref, mixb_ref,
                   poolw_ref, pscale_ref,
                   outa_ref, outb_ref, q_ref, k_ref, v_ref, gi_ref, gf_ref, vn_ref, pool_ref,
                   pooled_scr):
    h = _rms(x_ref[...], gpre_ref[...]).astype(BF16)
    _branch_a(h, wcat_ref, lng_ref, lnb_ref, mixm_ref, mixb_ref, outa_ref, vn_out_ref=vn_ref)

    def hist_out(s, val):
        pool_ref[s] = val

    _branch_b(h, wcat_ref, poolw_ref, pscale_ref, lambda s: hist_ref[s], hist_out, pooled_scr, outb_ref,
              rs=SAMPLE_SEQS_PER_BLOCK, pos1=None)
    q = _dot(h, wcat_ref[:, OFF_Q:OFF_Q + 512])
    k = _dot(h, wcat_ref[:, OFF_K:OFF_K + 512]) * (DQK ** -0.5)
    v = _dot(h, wcat_ref[:, OFF_V:OFF_V + D_MODEL])
    for j in range(512 // 128):
        q_ref[j] = q[:, j * 128:(j + 1) * 128]
        k_ref[j] = k[:, j * 128:(j + 1) * 128]
    for j in range(D_MODEL // 128):
        v_ref[j] = v[:, j * 128:(j + 1) * 128]
    gi_ref[...] = _dot(h, wifc_ref[:, 0:128])
    gf_ref[...] = _dot(h, wifc_ref[:, 128:256])


def _mix_s2_kernel(q_ref, k_ref, v_ref, gi_ref, gf_ref, ib_ref, fb_ref, normg_ref, c_ref, n_ref, m_ref,
                   cbuf_ref, hc_ref, cn_ref, nn_ref, mn_ref,
                   cq_scr, win_scr, dec_scr):
    del cbuf_ref
    nb = SAMPLE_SEQS_PER_BLOCK
    steps = SAMPLE_STEPS

    vblocks = DV // 128

    def slab(ref, t):
        return ref[t * nb:(t + 1) * nb, :]

    def slab_v(ref, hd, t):
        return jnp.concatenate([ref[hd * vblocks + j, t * nb:(t + 1) * nb, :] for j in range(vblocks)], axis=1)

    m0 = m_ref[...]
    ig = [slab(gi_ref, t) + ib_ref[...] for t in range(steps)]
    lf = [_log_sigmoid(slab(gf_ref, t) + fb_ref[...]) for t in range(steps)]
    b, a, m_t = [], [], []
    run_b = jnp.zeros_like(m0)
    run_max = jnp.full(m0.shape, -jnp.inf, F32)
    for t in range(steps):
        run_b = run_b + lf[t]
        b.append(run_b)
        a.append(ig[t] - run_b)
        run_max = jnp.maximum(run_max, a[t])
        m_t.append(run_b + jnp.maximum(m0, run_max))
    inter = [jnp.exp(b[t] + m0 - m_t[t]) for t in range(steps)]
    m_end = m_t[-1]
    decay = jnp.exp(b[-1] + m0 - m_end)
    w_in = [jnp.exp(a[t] + b[-1] - m_end) for t in range(steps)]
    mn_ref[...] = m_end
    for t in range(steps):
        win_scr[t * nb:(t + 1) * nb, :] = w_in[t]
        dec_scr[t * nb:(t + 1) * nb, :] = decay

    def seq_body(bi, carry):
        rows = pl.ds(bi, steps, stride=nb)
        wcols = win_scr[rows, :]
        dcols = dec_scr[rows, :]
        for hd in range(HEADS):
            c_old = c_ref[bi, hd]
            qb = q_ref[hd, rows, :].astype(BF16)
            cq = _dot_nt(qb, c_old.astype(BF16))
            for j in range(vblocks):
                cq_scr[hd * vblocks + j, rows, :] = cq[:, j * 128:(j + 1) * 128]
            vrows = jnp.concatenate([v_ref[hd * vblocks + j, rows, :] for j in range(vblocks)], axis=1)
            vs = (vrows * wcols[:, hd:hd + 1]).astype(BF16)
            kb = k_ref[hd, rows, :].astype(BF16)
            cn_ref[bi, hd] = dcols[0:1, hd:hd + 1] * c_old + _dot_tn(vs, kb)
        return carry

    lax.fori_loop(0, nb, seq_body, 0)

    for hd in range(HEADS):
        qcols = slice(hd * DQK, (hd + 1) * DQK)
        vcols = slice(hd * DV, (hd + 1) * DV)
        col = slice(hd, hd + 1)
        qs = [slab(q_ref.at[hd], t) for t in range(steps)]
        ks = [slab(k_ref.at[hd], t) for t in range(steps)]
        vs = [slab_v(v_ref, hd, t) for t in range(steps)]
        n_old = n_ref[:, qcols]
        n_new = decay[:, col] * n_old
        for t in range(steps):
            n_new = n_new + w_in[t][:, col] * ks[t]
        nn_ref[:, qcols] = n_new
        for t in range(steps):
            num = inter[t][:, col] * slab_v(cq_scr, hd, t)
            den = inter[t][:, col] * jnp.sum(qs[t] * n_old, axis=1, keepdims=True)
            bm = b[t][:, col] - m_t[t][:, col]
            for s in range(t + 1):
                wts = jnp.sum(qs[t] * ks[s], axis=1, keepdims=True) * jnp.exp(bm + a[s][:, col])
                num = num + wts * vs[s]
                den = den + wts
            hout = num / jnp.maximum(jnp.abs(den), jnp.exp(-m_t[t][:, col]))
            hc_ref[t * nb:(t + 1) * nb, vcols] = _rms(hout, normg_ref[:, vcols])


def _mix_s3_kernel(x_ref, outa_ref, outb_ref, hc_ref, gpre_ref, wcat_ref, wg_ref, pa_ref, pb_ref, pc_ref,
                   wout_ref, gpost_ref, o_ref):
    x = x_ref[...]
    h = _rms(x, gpre_ref[...]).astype(BF16)
    o_ref[...] = _merge(x, h, hc_ref[...], wcat_ref, wg_ref, pa_ref, pb_ref, pc_ref, wout_ref, gpost_ref,
                        outa_ref, outb_ref)


def _mix_sample(xs, hist, c_all, c_buf, n_in, m_in, lw, layer):
    rows = xs.shape[0]
    nblk = rows // BLOCK_ROWS
    hist_rows = POOL_HIST * SAMPLE_SEQS_PER_BLOCK
    cs1 = functools.partial(_const_spec, index=layer, ngrid=1)
    full = lambda shape: pl.BlockSpec(shape, lambda i: (0,) * len(shape))
    params = pltpu.CompilerParams(dimension_semantics=("arbitrary",), vmem_limit_bytes=VMEM_LIMIT_BYTES)

    w1 = [lw['mix_norm_pre'], lw['wcat'], lw['wifc'], lw['ln_g'], lw['ln_b'], lw['mixm_s'], lw['mixb_s'],
          lw['pool_w'], lw['pool_scale']]
    s1_shapes = (
        jax.ShapeDtypeStruct((rows, 512), BF16),
        jax.ShapeDtypeStruct((rows, 512), BF16),
        jax.ShapeDtypeStruct((HEADS, rows, DQK), F32),
        jax.ShapeDtypeStruct((HEADS, rows, DQK), F32),
        jax.ShapeDtypeStruct((D_MODEL // 128, rows, 128), F32),
        jax.ShapeDtypeStruct((rows, 128), F32),
        jax.ShapeDtypeStruct((rows, 128), F32),
        jax.ShapeDtypeStruct((rows, 512), F32),
        jax.ShapeDtypeStruct((nblk, hist_rows, 512), F32),
    )
    outa, outb, q, k, v, gi, gf, vn, pool_new = pl.pallas_call(
        _mix_s1_kernel,
        grid=(1,),
        in_specs=[full(xs.shape), full(hist.shape)] + [cs1(w.shape) for w in w1],
        out_specs=tuple(full(s.shape) for s in s1_shapes),
        out_shape=s1_shapes,
        scratch_shapes=[pltpu.VMEM((rows, 512), BF16)],
        compiler_params=params,
        name="mix_s1",
    )(xs, hist, *w1)

    nb = SAMPLE_SEQS_PER_BLOCK
    blk = lambda width: pl.BlockSpec((BLOCK_ROWS, width), lambda j: (j, 0))
    seqblk = lambda width: pl.BlockSpec((nb, width), lambda j: (j, 0))
    cblk = pl.BlockSpec((None, nb, HEADS, DV, DQK), lambda j: (layer, j, 0, 0, 0))
    lrow = lambda width: pl.BlockSpec((None, 1, width), lambda j: (layer, 0, 0))
    colblk = lambda n: pl.BlockSpec((n, BLOCK_ROWS, 128), lambda j: (0, j, 0))
    hc, c_buf, n_new, m_new = pl.pallas_call(
        _mix_s2_kernel,
        grid=(nblk,),
        in_specs=[colblk(HEADS), colblk(HEADS), colblk(D_MODEL // 128), blk(128), blk(128),
                  lrow(128), lrow(128), lrow(D_MODEL), cblk, seqblk(512), seqblk(128),
                  pl.BlockSpec(memory_space=pl.ANY)],
        out_specs=(blk(D_MODEL), cblk, seqblk(512), seqblk(128)),
        out_shape=(jax.ShapeDtypeStruct((rows, D_MODEL), F32),
                   jax.ShapeDtypeStruct(c_buf.shape, F32),
                   jax.ShapeDtypeStruct(n_in.shape, F32),
                   jax.ShapeDtypeStruct(m_in.shape, F32)),
        input_output_aliases={11: 1},
        scratch_shapes=[pltpu.VMEM((D_MODEL // 128, BLOCK_ROWS, 128), F32), pltpu.VMEM((BLOCK_ROWS, 128), F32),
                        pltpu.VMEM((BLOCK_ROWS, 128), F32)],
        compiler_params=pltpu.CompilerParams(dimension_semantics=("parallel",),
                                             vmem_limit_bytes=VMEM_LIMIT_BYTES),
        name="mix_s2",
    )(q, k, v, gi, gf, lw['ib'], lw['fb'], lw['norm_g'], c_all, n_in, m_in, c_buf)

    w3 = [lw['mix_norm_pre'], lw['wcat'], lw['wg'], lw['proj_a'], lw['proj_b'], lw['proj_c'], lw['w_out'],
          lw['mix_norm_post']]
    y = pl.pallas_call(
        _mix_s3_kernel,
        grid=(1,),
        in_specs=[full(xs.shape), full(outa.shape), full(outb.shape), full(hc.shape)]
        + [cs1(w.shape) for w in w3],
        out_specs=full(xs.shape),
        out_shape=jax.ShapeDtypeStruct(xs.shape, F32),
        compiler_params=params,
        name="mix_s3",
    )(xs, outa, outb, hc, *w3)
    return y, vn, pool_new, c_buf, n_new, m_new


def _to_blocked(x):
    nseq, t, c = x.shape
    nb = SAMPLE_SEQS_PER_BLOCK
    return x.reshape(nseq // nb, nb, t, c).transpose(0, 2, 1, 3).reshape(nseq * t, c)


def _from_blocked(x, t):
    rows, c = x.shape
    nb = SAMPLE_SEQS_PER_BLOCK
    nseq = rows // t
    return x.reshape(nseq // nb, t, nb, c).transpose(0, 2, 1, 3).reshape(nseq, t, c)


def _prep_weights(p):
    w_in = p['w_in']
    row = lambda a: a.reshape(DEPTH, 1, -1).astype(F32)
    wi = w_in[:, :, 4608:4612]
    wf = w_in[:, :, 4612:4616]
    wift = jnp.zeros((DEPTH, 32, D_MODEL), F32)
    wift = wift.at[:, 0:HEADS].set(wi.transpose(0, 2, 1)).at[:, 16:16 + HEADS].set(wf.transpose(0, 2, 1))
    wifc = jnp.zeros((DEPTH, D_MODEL, 256), F32)
    wifc = wifc.at[:, :, 0:HEADS].set(wi).at[:, :, 128:128 + HEADS].set(wf)
    gbias = jnp.zeros((DEPTH, 32, BLOCK_ROWS), F32)
    gbias = gbias.at[:, 0:HEADS].set(jnp.broadcast_to(p['mlstm_i_bias'][:, :, None], (DEPTH, HEADS, BLOCK_ROWS)))
    gbias = gbias.at[:, 16:16 + HEADS].set(
        jnp.broadcast_to(p['mlstm_f_bias'][:, :, None], (DEPTH, HEADS, BLOCK_ROWS)))
    lane_pad = lambda a: jnp.zeros((DEPTH, 1, 128), F32).at[:, 0, 0:HEADS].set(a)

    ws = p['gmlp_w_s']
    bs = p['gmlp_b_s']
    mixm_p = jnp.tril(ws)
    mixb_p = jnp.broadcast_to(bs[..., None], ws.shape)
    nb = SAMPLE_SEQS_PER_BLOCK
    tril_s = jnp.tril(ws[:, :, :SAMPLE_STEPS, :SAMPLE_STEPS])
    eye = jnp.eye(nb, dtype=F32)
    mixm_s = jnp.einsum('lgts,bc->lgtbsc', tril_s, eye).reshape(DEPTH, N_GROUPS, BLOCK_ROWS, BLOCK_ROWS)
    mixb_s = jnp.broadcast_to(jnp.repeat(bs[:, :, :SAMPLE_STEPS], nb, axis=2)[..., None], ws.shape)

    return dict(
        ffn1_norm_pre=row(p['ffn1_norm_pre']), ffn1_w_in=p['ffn1_w_in'],
        ffn1_w_down=p['ffn1_w_down'], ffn1_norm_post=row(p['ffn1_norm_post']),
        ffn2_norm_pre=row(p['ffn2_norm_pre']), ffn2_w_in=p['ffn2_w_in'],
        ffn2_w_down=p['ffn2_w_down'], ffn2_norm_post=row(p['ffn2_norm_post']),
        mix_norm_pre=row(p['mix_norm_pre']), mix_norm_post=row(p['mix_norm_post']),
        wcat=w_in[:, :, :WCAT_COLS].astype(BF16), wg=w_in[:, :, WCAT_COLS + GATE_COLS:].astype(BF16),
        wift=wift.astype(BF16), wifc=wifc.astype(BF16), gbias=gbias,
        ib=lane_pad(p['mlstm_i_bias']), fb=lane_pad(p['mlstm_f_bias']),
        ln_g=row(p['gmlp_ln_g']), ln_b=row(p['gmlp_ln_b']),
        mixm_p=mixm_p.astype(BF16), mixb_p=mixb_p.astype(F32),
        mixm_s=mixm_s.astype(BF16), mixb_s=mixb_s.astype(F32),
        pool_w=p['pool_w'].astype(BF16), pool_scale=row(p['pool_scale']), norm_g=row(p['mlstm_norm_g']),
        proj_a=p['proj_a'].astype(BF16), proj_b=p['proj_b'].astype(BF16), proj_c=p['proj_c'].astype(BF16),
        w_out=p['w_out'].astype(BF16),
    )


def kernel(x_prompt, x_sample, state_pool, state_mlstm_C, state_mlstm_n, state_mlstm_m, ffn1_norm_pre, ffn1_w_in, ffn1_w_down, ffn1_norm_post, mix_norm_pre, w_in, gmlp_ln_g, gmlp_ln_b, gmlp_w_s, gmlp_b_s, pool_w, pool_scale, mlstm_i_bias, mlstm_f_bias, mlstm_norm_g, proj_a, proj_b, proj_c, w_out, mix_norm_post, ffn2_norm_pre, ffn2_w_in, ffn2_w_down, ffn2_norm_post):
    params = dict(ffn1_norm_pre=ffn1_norm_pre, ffn1_w_in=ffn1_w_in, ffn1_w_down=ffn1_w_down,
                  ffn1_norm_post=ffn1_norm_post, mix_norm_pre=mix_norm_pre, w_in=w_in, gmlp_ln_g=gmlp_ln_g,
                  gmlp_ln_b=gmlp_ln_b, gmlp_w_s=gmlp_w_s, gmlp_b_s=gmlp_b_s, pool_w=pool_w,
                  pool_scale=pool_scale, mlstm_i_bias=mlstm_i_bias, mlstm_f_bias=mlstm_f_bias,
                  mlstm_norm_g=mlstm_norm_g, proj_a=proj_a, proj_b=proj_b, proj_c=proj_c, w_out=w_out,
                  mix_norm_post=mix_norm_post, ffn2_norm_pre=ffn2_norm_pre, ffn2_w_in=ffn2_w_in,
                  ffn2_w_down=ffn2_w_down, ffn2_norm_post=ffn2_norm_post)
    lw = _prep_weights(params)
    nbp, seq, _ = x_prompt.shape
    nbs, steps, _ = x_sample.shape

    yp = x_prompt
    ys = _to_blocked(x_sample)
    hist_all = jnp.pad(state_pool, ((0, 0), (0, 0), (1, 0), (0, 0)))
    n_all = state_mlstm_n.reshape(DEPTH, nbs, HEADS * DQK)
    m_all = jnp.pad(state_mlstm_m, ((0, 0), (0, 0), (0, 128 - HEADS)))

    pool_p, c_p, n_p, m_p = [], [], [], []
    pool_s, n_s, m_s, v_s = [], [], [], []
    c_sample = jnp.zeros(state_mlstm_C.shape, F32)
    for l in range(DEPTH):
        ffn1 = (lw['ffn1_norm_pre'], lw['ffn1_w_in'], lw['ffn1_w_down'], lw['ffn1_norm_post'])
        ffn2 = (lw['ffn2_norm_pre'], lw['ffn2_w_in'], lw['ffn2_w_down'], lw['ffn2_norm_post'])
        yp = _ffn(yp.reshape(nbp * seq, D_MODEL), *ffn1, l).reshape(nbp, seq, D_MODEL)
        yp, pb, cf, nf_, mf = _mix_prompt(yp, lw, l)
        yp = _ffn(yp.reshape(nbp * seq, D_MODEL), *ffn2, l).reshape(nbp, seq, D_MODEL)
        pool_p.append(pb[:, 1:, :])
        c_p.append(cf)
        n_p.append(nf_[:, :HEADS, :])
        m_p.append(mf[:, :HEADS, 0])
        ys = _ffn(ys, *ffn1, l)
        hist = _to_blocked(hist_all[l]).reshape(nbs // SAMPLE_SEQS_PER_BLOCK,
                                                POOL_HIST * SAMPLE_SEQS_PER_BLOCK, 512)
        ys, vn, pool_new, c_sample, n_new, m_new = _mix_sample(
            ys, hist, state_mlstm_C, c_sample, n_all[l], m_all[l], lw, l)
        ys = _ffn(ys, *ffn2, l)
        pool_s.append(_from_blocked(pool_new.reshape(nbs * POOL_HIST, 512), POOL_HIST)[:, 1:, :])
        n_s.append(n_new.reshape(nbs, HEADS, DQK))
        m_s.append(m_new[:, :HEADS])
        v_s.append(_from_blocked(vn, steps))
    return (yp, _from_blocked(ys, steps),
            jnp.stack(pool_p), jnp.stack(c_p), jnp.stack(n_p), jnp.stack(m_p),
            jnp.stack(pool_s), c_sample, jnp.stack(n_s), jnp.stack(m_s), jnp.stack(v_s))
```

```python
import functools

import jax
import jax.numpy as jnp
from jax import lax
from jax.experimental import pallas as pl
from jax.experimental.pallas import tpu as pltpu

F32 = jnp.float32
BF16 = jnp.bfloat16

D_MODEL = 1024
D_FF = 2816
DEPTH = 4
EPS = 1e-6
PAST_LEN = 16384
N_GROUPS = 4
GROUP_W = 128
POOL_WINDOWS = (2, 4, 8, 16)
POOL_HIST = 16
HEADS = 4
DQK = 128
DV = 256
BLOCK_ROWS = 128
SAMPLE_SEQS_PER_BLOCK = 16
SAMPLE_STEPS = 8

OFF_AU, OFF_AV, OFF_B, OFF_Q, OFF_K, OFF_V, OFF_O = 0, 512, 1024, 1536, 2048, 2560, 3584
WCAT_COLS = 4608
GATE_COLS = 8

SAMPLE_SEQ_UNROLL = 4

FFN_TM = 2048
FFN_TF = 256
MIX_ROW_SPLIT = 2
FFN_ROW_BLOCKS = 4
PROMPT_SEQS_PER_STEP = 4
VMEM_LIMIT_BYTES = 56 * 1024 * 1024


def _dot(a, b):
    return jnp.dot(a, b, preferred_element_type=F32)


def _dot_nt(a, b):
    return lax.dot_general(a, b, (((1,), (1,)), ((), ())), preferred_element_type=F32)


def _dot_tn(a, b):
    return lax.dot_general(a, b, (((0,), (0,)), ((), ())), preferred_element_type=F32)


def _rms(x, g):
    return x * lax.rsqrt(jnp.mean(x * x, axis=-1, keepdims=True) + EPS) * g


def _sigmoid(x):
    return 1.0 / (1.0 + jnp.exp(-x))


def _gelu(x):
    return 0.5 * x * (1.0 + jnp.tanh(0.7978845608028654 * (x + 0.044715 * (x * x * x))))


def _log_sigmoid(x):
    return jnp.minimum(x, 0.0) - jnp.log1p(jnp.exp(-jnp.abs(x)))


def _const_spec(shape, index, ngrid):
    zeros = (0,) * (len(shape) - 1)
    if ngrid == 1:
        imap = lambda i: (index,) + zeros
    else:
        imap = lambda i, j: (index,) + zeros
    return pl.BlockSpec((None,) + tuple(shape[1:]), imap, pipeline_mode=pl.Buffered(1))


def _ffn_kernel(x_ref, gpre_ref, wgate_ref, wup_ref, wdown_ref, gpost_ref, o_ref, h_scr):
    f = pl.program_id(1)
    last = pl.num_programs(1) - 1
    tm = x_ref.shape[0]
    rb = tm // FFN_ROW_BLOCKS

    def step(first, final):
        wgate = wgate_ref[...].astype(BF16)
        wup = wup_ref[...].astype(BF16)
        wdown = wdown_ref[...].astype(BF16)
        for r in range(FFN_ROW_BLOCKS):
            rows = slice(r * rb, (r + 1) * rb)
            if first:
                h = _rms(x_ref[rows, :], gpre_ref[...]).astype(BF16)
                h_scr[rows, :] = h
            else:
                h = h_scr[rows, :]
            gate = _dot(h, wgate)
            up = _dot(h, wup)
            act = (gate * _sigmoid(gate) * up).astype(BF16)
            part = _dot(act, wdown)
            acc = part if first else o_ref[rows, :] + part
            if final:
                acc = x_ref[rows, :] + _rms(acc, 0.5 * gpost_ref[...])
            o_ref[rows, :] = acc

    @pl.when(f == 0)
    def _():
        step(True, False)

    @pl.when(jnp.logical_and(f > 0, f < last))
    def _():
        step(False, False)

    @pl.when(f == last)
    def _():
        step(False, True)


def _ffn(x2d, gpre, w_in, wdown, gpost, layer):
    m = x2d.shape[0]
    tm = min(FFN_TM, m)
    nf = D_FF // FFN_TF
    return pl.pallas_call(
        _ffn_kernel,
        grid=(m // tm, nf),
        in_specs=[
            pl.BlockSpec((tm, D_MODEL), lambda i, f: (i, 0), pipeline_mode=pl.Buffered(1)),
            pl.BlockSpec((None, 1, D_MODEL), lambda i, f: (layer, 0, 0)),
            pl.BlockSpec((None, D_MODEL, FFN_TF), lambda i, f: (layer, 0, f)),
            pl.BlockSpec((None, D_MODEL, FFN_TF), lambda i, f: (layer, 0, f + nf)),
            pl.BlockSpec((None, FFN_TF, D_MODEL), lambda i, f: (layer, f, 0)),
            pl.BlockSpec((None, 1, D_MODEL), lambda i, f: (layer, 0, 0)),
        ],
        out_specs=pl.BlockSpec((tm, D_MODEL), lambda i, f: (i, 0)),
        out_shape=jax.ShapeDtypeStruct((m, D_MODEL), F32),
        scratch_shapes=[pltpu.VMEM((tm, D_MODEL), BF16)],
        compiler_params=pltpu.CompilerParams(
            dimension_semantics=("parallel", "arbitrary"), vmem_limit_bytes=VMEM_LIMIT_BYTES),
        name="ffn",
    )(x2d, gpre, w_in, w_in, wdown, gpost)


def _prenorm(x, g):
    step = x.shape[0] // MIX_ROW_SPLIT
    pieces = [_rms(x[i * step:(i + 1) * step, :], g).astype(BF16) for i in range(MIX_ROW_SPLIT)]
    return jnp.concatenate(pieces, axis=0), pieces


def _rows_dot(pieces, w):
    return jnp.concatenate([_dot(p, w) for p in pieces], axis=0)


def _branch_a(h, h_pieces, wcat_ref, lng_ref, lnb_ref, mixm_ref, mixb_ref, outa_scr, vn_out_ref=None):
    rows = h.shape[0]
    a_v = _gelu(_rows_dot(h_pieces, wcat_ref[:, OFF_AV:OFF_AV + 512]))
    mu = jnp.mean(a_v, axis=-1, keepdims=True)
    xc = a_v - mu
    var = jnp.mean(xc * xc, axis=-1, keepdims=True)
    v_n = xc * lax.rsqrt(var + EPS) * lng_ref[...] + lnb_ref[...]
    if vn_out_ref is not None:
        vn_out_ref[...] = v_n
    vb = v_n.astype(BF16)
    a_u = _gelu(_dot(h, wcat_ref[:, OFF_AU:OFF_AU + 512]))
    for g in range(N_GROUPS):
        cols = slice(g * GROUP_W, (g + 1) * GROUP_W)
        mix_g = mixm_ref[g]
        bias_g = mixb_ref[g]
        for s in range(rows // BLOCK_ROWS):
            rws = slice(s * BLOCK_ROWS, (s + 1) * BLOCK_ROWS)
            mixed = _dot(mix_g, vb[rws, cols]) + bias_g
            outa_scr[rws, cols] = (a_u[rws, cols] * mixed).astype(BF16)


def _branch_b(h, wcat_ref, poolw_ref, pscale_ref, hist_in, hist_out, pooled_scr, outb_scr, *, rs, pos1):
    rows = h.shape[0]
    hist_rows = POOL_HIST * rs
    b_in = _dot(h, wcat_ref[:, OFF_B:OFF_B + 512])
    for s in range(rows // BLOCK_ROWS):
        rws = slice(s * BLOCK_ROWS, (s + 1) * BLOCK_ROWS)
        cur = b_in[rws, :]
        ext = jnp.concatenate([hist_in(s), cur], axis=0)
        for g, w in enumerate(POOL_WINDOWS):
            cols = slice(g * GROUP_W, (g + 1) * GROUP_W)
            acc = ext[:, cols]
            sh = 1
            while sh < w:
                acc = acc + pltpu.roll(acc, sh * rs, axis=0)
                sh *= 2
            win = acc[hist_rows:, :]
            if pos1 is None:
                mean = win * (1.0 / w)
            else:
                mean = win / jnp.minimum(pos1, w).astype(F32)
            pooled_scr[rws, cols] = (mean - cur[:, cols]).astype(BF16)
        hist_out(s, ext[BLOCK_ROWS:, :])
    for g in range(N_GROUPS):
        cols = slice(g * GROUP_W, (g + 1) * GROUP_W)
        mixed = _dot(pooled_scr[:, cols], poolw_ref[g]) * pscale_ref[:, cols]
        outb_scr[:, cols] = mixed.astype(BF16)


def _merge(x, h, h_pieces, hc, wcat_ref, wg_ref, pa_ref, pb_ref, pc_ref, wout_ref, gpost_ref, outa_scr, outb_scr):
    merged = _sigmoid(_rows_dot(h_pieces, wg_ref[:, 0:D_MODEL])) * _dot(outa_scr[...], pa_ref[...])
    merged += _sigmoid(_dot(h, wg_ref[:, D_MODEL:2 * D_MODEL])) * _dot(outb_scr[...], pb_ref[...])
    out_c = (_sigmoid(_dot(h, wcat_ref[:, OFF_O:OFF_O + D_MODEL])) * hc).astype(BF16)
    merged += _sigmoid(_dot(h, wg_ref[:, 2 * D_MODEL:3 * D_MODEL])) * _dot(out_c, pc_ref[...])
    mb = merged.astype(BF16)
    step = x.shape[0] // MIX_ROW_SPLIT
    outs = []
    for i in range(MIX_ROW_SPLIT):
        rws = slice(i * step, (i + 1) * step)
        outs.append(x[rws, :] + _rms(_dot(mb[rws, :], wout_ref[...]), gpost_ref[...]))
    return jnp.concatenate(outs, axis=0)


def _scan_lanes(x, op, fill):
    lane = lax.broadcasted_iota(jnp.int32, x.shape, 1)
    sh = 1
    while sh < x.shape[1]:
        x = op(x, jnp.where(lane >= sh, pltpu.roll(x, sh, axis=1), fill))
        sh *= 2
    return x


def _last_lane(x):
    lane = lax.broadcasted_iota(jnp.int32, x.shape, 1)
    return jnp.max(jnp.where(lane == x.shape[1] - 1, x, -jnp.inf), axis=1, keepdims=True)


def _mlstm_gates(h_pieces, wift_ref, gbias_ref, m_ref, *, nseq):
    t_len = BLOCK_ROWS
    grow = jnp.concatenate([_dot_nt(wift_ref[...], p) for p in h_pieces], axis=1)
    seqs = range(nseq)
    ig = jnp.concatenate([grow[0:16, s * t_len:(s + 1) * t_len] + gbias_ref[0:16, :] for s in seqs], axis=0)
    lf = _log_sigmoid(
        jnp.concatenate([grow[16:32, s * t_len:(s + 1) * t_len] + gbias_ref[16:32, :] for s in seqs], axis=0))
    m0 = m_ref[...].reshape(nseq * 16, t_len)
    b = _scan_lanes(lf, jnp.add, 0.0)
    a = ig - b
    m_t = b + jnp.maximum(m0, _scan_lanes(a, jnp.maximum, -jnp.inf))
    inter = jnp.exp(b + m0 - m_t)
    b_last = _last_lane(b)
    m_end = _last_lane(m_t)
    decay = jnp.exp(b_last + m0 - m_end)
    w_in = jnp.exp(a + b_last - m_end)
    m_ref[...] = jnp.broadcast_to(m_end, (nseq * 16, t_len)).reshape(nseq, 16, t_len)
    stacked = jnp.concatenate([b - m_t, inter, jnp.exp(-m_t), w_in], axis=0)
    pad = (-stacked.shape[0]) % t_len
    if pad:
        stacked = jnp.concatenate([stacked, jnp.zeros((pad, t_len), F32)], axis=0)
    cols_t = [stacked[i * t_len:(i + 1) * t_len, :].T for i in range(stacked.shape[0] // t_len)]

    def col(quantity, s, hd):
        idx = quantity * nseq * 16 + s * 16 + hd
        return cols_t[idx // t_len][:, idx % t_len:idx % t_len + 1]

    return a, decay, col


def _mlstm_prompt(h, gates, wcat_ref, normg_ref, c_ref, n_ref, hc_scr, *, nseq):
    t_len = BLOCK_ROWS
    a_all, decay_all, col = gates
    q = _dot(h, wcat_ref[:, OFF_Q:OFF_Q + 512])
    k = _dot(h, wcat_ref[:, OFF_K:OFF_K + 512]) * (DQK ** -0.5)
    v = _dot(h, wcat_ref[:, OFF_V:OFF_V + D_MODEL])
    ri = lax.broadcasted_iota(jnp.int32, (t_len, t_len), 0)
    ci = lax.broadcasted_iota(jnp.int32, (t_len, t_len), 1)
    causal = ci <= ri
    for s in range(nseq):
        rws = slice(s * t_len, (s + 1) * t_len)
        a = a_all[s * 16:(s + 1) * 16, :]
        decay = decay_all[s * 16:(s + 1) * 16, :]
        for hd in range(HEADS):
            qh = q[rws, hd * DQK:(hd + 1) * DQK]
            kh = k[rws, hd * DQK:(hd + 1) * DQK]
            vh = v[rws, hd * DV:(hd + 1) * DV]
            qb = qh.astype(BF16)
            kb = kh.astype(BF16)
            bm_c = col(0, s, hd)
            inter_c = col(1, s, hd)
            en_c = col(2, s, hd)
            win_c = col(3, s, hd)
            dmat = jnp.exp(jnp.where(causal, bm_c + a[hd:hd + 1, :], -jnp.inf))
            wmat = _dot_nt(qb, kb) * dmat
            c_old = c_ref[s, hd]
            n_old = n_ref[s, hd:hd + 1, :]
            num = _dot(wmat.astype(BF16), vh.astype(BF16)) + inter_c * _dot_nt(qb, c_old.astype(BF16))
            den = jnp.sum(wmat, axis=1, keepdims=True) + inter_c * jnp.sum(qh * n_old, axis=1, keepdims=True)
            hout = num / jnp.maximum(jnp.abs(den), en_c)
            hc_scr[rws, hd * DV:(hd + 1) * DV] = _rms(hout, normg_ref[:, hd * DV:(hd + 1) * DV])
            dec_row = decay[hd:hd + 1, :]
            c_ref[s, hd] = dec_row * c_old + _dot_tn((vh * win_c).astype(BF16), kb)
            n_ref[s, hd:hd + 1, :] = dec_row * n_old + jnp.sum(kh * win_c, axis=0, keepdims=True)


def _mix_prompt_kernel(x_ref, gpre_ref, wcat_ref, wg_ref, wift_ref, gbias_ref, lng_ref, lnb_ref, mixm_ref, mixb_ref,
                       poolw_ref, pscale_ref, normg_ref, pa_ref, pb_ref, pc_ref, wout_ref, gpost_ref,
                       o_ref, pool_ref, c_ref, n_ref, m_ref,
                       outa_scr, outb_scr, pooled_scr, hc_scr, *, nseq):
    c = pl.program_id(1)
    rows = nseq * BLOCK_ROWS

    @pl.when(c == 0)
    def _():
        pool_ref[...] = jnp.zeros(pool_ref.shape, F32)
        c_ref[...] = jnp.zeros(c_ref.shape, F32)
        n_ref[...] = jnp.zeros(n_ref.shape, F32)
        m_ref[...] = jnp.zeros(m_ref.shape, F32)

    x = x_ref[...].reshape(rows, D_MODEL)
    h, h_pieces = _prenorm(x, gpre_ref[...])

    gates = _mlstm_gates(h_pieces, wift_ref, gbias_ref, m_ref, nseq=nseq)

    _branch_a(h, h_pieces, wcat_ref, lng_ref, lnb_ref, mixm_ref, mixb_ref, outa_scr)

    pos1 = c * BLOCK_ROWS + 1 + lax.broadcasted_iota(jnp.int32, (BLOCK_ROWS, GROUP_W), 0)

    def hist_out(s, val):
        pool_ref[s] = val

    _branch_b(h, wcat_ref, poolw_ref, pscale_ref, lambda s: pool_ref[s], hist_out, pooled_scr, outb_scr,
              rs=1, pos1=pos1)

    _mlstm_prompt(h, gates, wcat_ref, normg_ref, c_ref, n_ref, hc_scr, nseq=nseq)

    y = _merge(x, h, [h], hc_scr[...], wcat_ref, wg_ref, pa_ref, pb_ref, pc_ref, wout_ref, gpost_ref,
               outa_scr, outb_scr)
    o_ref[...] = y.reshape(nseq, BLOCK_ROWS, D_MODEL)


def _mix_prompt(x3d, lw, layer):
    nb, seq, _ = x3d.shape
    nseq = PROMPT_SEQS_PER_STEP
    rows = nseq * BLOCK_ROWS
    cs = functools.partial(_const_spec, index=layer, ngrid=2)
    weights = [lw['mix_norm_pre'], lw['wcat'], lw['wg'], lw['wift'], lw['gbias'], lw['ln_g'], lw['ln_b'],
               lw['mixm_p'], lw['mixb_p'], lw['pool_w'], lw['pool_scale'], lw['norm_g'],
               lw['proj_a'], lw['proj_b'], lw['proj_c'], lw['w_out'], lw['mix_norm_post']]
    out_shapes = (
        jax.ShapeDtypeStruct((nb, seq, D_MODEL), F32),
        jax.ShapeDtypeStruct((nb, POOL_HIST, 512), F32),
        jax.ShapeDtypeStruct((nb, HEADS, DV, DQK), F32),
        jax.ShapeDtypeStruct((nb, 8, DQK), F32),
        jax.ShapeDtypeStruct((nb, 16, BLOCK_ROWS), F32),
    )
    out_specs = (
        pl.BlockSpec((nseq, BLOCK_ROWS, D_MODEL), lambda g, c: (g, c, 0)),
        pl.BlockSpec((nseq, POOL_HIST, 512), lambda g, c: (g, 0, 0)),
        pl.BlockSpec((nseq, HEADS, DV, DQK), lambda g, c: (g, 0, 0, 0)),
        pl.BlockSpec((nseq, 8, DQK), lambda g, c: (g, 0, 0)),
        pl.BlockSpec((nseq, 16, BLOCK_ROWS), lambda g, c: (g, 0, 0)),
    )
    return pl.pallas_call(
        functools.partial(_mix_prompt_kernel, nseq=nseq),
        grid=(nb // nseq, seq // BLOCK_ROWS),
        in_specs=[pl.BlockSpec((nseq, BLOCK_ROWS, D_MODEL), lambda g, c: (g, c, 0))]
        + [cs(w.shape) for w in weights],
        out_specs=out_specs,
        out_shape=out_shapes,
        scratch_shapes=[pltpu.VMEM((rows, 512), BF16), pltpu.VMEM((rows, 512), BF16),
                        pltpu.VMEM((rows, 512), BF16), pltpu.VMEM((rows, D_MODEL), F32)],
        compiler_params=pltpu.CompilerParams(
            dimension_semantics=("parallel", "arbitrary"), vmem_limit_bytes=VMEM_LIMIT_BYTES),
        name="mix_prompt",
    )(x3d, *weights)


def _mix_s1_kernel(x_ref, hist_ref, gpre_ref, wcat_ref, wifc_ref, lng_ref, lnb_ref, mixm_ref, mixb_ref,
                   poolw_ref, pscale_ref,
                   outa_ref, outb_ref, q_ref, k_ref, v_ref, gi_ref, gf_ref, vn_ref, pool_ref,
                   pooled_scr):
    h, h_pieces = _prenorm(x_ref[...], gpre_ref[...])
    _branch_a(h, h_pieces, wcat_ref, lng_ref, lnb_ref, mixm_ref, mixb_ref, outa_ref, vn_out_ref=vn_ref)

    def hist_out(s, val):
        pool_ref[s] = val

    _branch_b(h, wcat_ref, poolw_ref, pscale_ref, lambda s: hist_ref[s], hist_out, pooled_scr, outb_ref,
              rs=SAMPLE_SEQS_PER_BLOCK, pos1=None)
    q = _dot(h, wcat_ref[:, OFF_Q:OFF_Q + 512])
    k = _dot(h, wcat_ref[:, OFF_K:OFF_K + 512]) * (DQK ** -0.5)
    v = _dot(h, wcat_ref[:, OFF_V:OFF_V + D_MODEL])
    for j in range(512 // 128):
        q_ref[j] = q[:, j * 128:(j + 1) * 128]
        k_ref[j] = k[:, j * 128:(j + 1) * 128]
    for j in range(D_MODEL // 128):
        v_ref[j] = v[:, j * 128:(j + 1) * 128]
    gi_ref[...] = _dot(h, wifc_ref[:, 0:HEADS * DQK])
    gf_ref[...] = _dot(h, wifc_ref[:, HEADS * DQK:2 * HEADS * DQK])


def _mix_s2_kernel(q_ref, k_ref, v_ref, gi_ref, gf_ref, ib_ref, fb_ref, normg_ref, c_ref, n_ref, m_ref,
                   cbuf_ref, hc_ref, cn_ref, nn_ref, mn_ref,
                   cq_scr, win_scr, dec_scr):
    del cbuf_ref
    nb = SAMPLE_SEQS_PER_BLOCK
    steps = SAMPLE_STEPS

    vblocks = DV // 128

    def slab(ref, t):
        return ref[t * nb:(t + 1) * nb, :]

    def slab_v(ref, hd, t):
        return jnp.concatenate([ref[hd * vblocks + j, t * nb:(t + 1) * nb, :] for j in range(vblocks)], axis=1)

    def twice(x):
        return jnp.concatenate([x] * vblocks, axis=1)

    m0 = m_ref[...]
    ig = [slab(gi_ref, t) + ib_ref[...] for t in range(steps)]
    lf = [_log_sigmoid(slab(gf_ref, t) + fb_ref[...]) for t in range(steps)]
    b, a, m_t = [], [], []
    run_b = jnp.zeros_like(m0)
    run_max = jnp.full(m0.shape, -jnp.inf, F32)
    for t in range(steps):
        run_b = run_b + lf[t]
        b.append(run_b)
        a.append(ig[t] - run_b)
        run_max = jnp.maximum(run_max, a[t])
        m_t.append(run_b + jnp.maximum(m0, run_max))
    inter = [jnp.exp(b[t] + m0 - m_t[t]) for t in range(steps)]
    m_end = m_t[-1]
    decay = jnp.exp(b[-1] + m0 - m_end)
    w_in = [jnp.exp(a[t] + b[-1] - m_end) for t in range(steps)]
    mn_ref[...] = m_end
    for hd in range(HEADS):
        hcols = slice(hd * DQK, (hd + 1) * DQK)
        for t in range(steps):
            win_scr[hd, t * nb:(t + 1) * nb, :] = w_in[t][:, hcols]
            dec_scr[hd, t * nb:(t + 1) * nb, :] = decay[:, hcols]

    def seq_body(bi, carry):
        rows = pl.ds(bi, steps, stride=nb)
        for hd in range(HEADS):
            c_old = c_ref[bi, hd]
            qb = q_ref[hd, rows, :].astype(BF16)
            cq = _dot_nt(qb, c_old.astype(BF16))
            for j in range(vblocks):
                cq_scr[hd * vblocks + j, rows, :] = cq[:, j * 128:(j + 1) * 128]
            vrows = jnp.concatenate([v_ref[hd * vblocks + j, rows, :] for j in range(vblocks)], axis=1)
            vs = (vrows * twice(win_scr[hd, rows, :])).astype(BF16)
            kb = k_ref[hd, rows, :].astype(BF16)
            dec_row = dec_scr[hd, rows, :][0:1, :]
            cn_ref[bi, hd] = dec_row * c_old + _dot_tn(vs, kb)
        return carry

    lax.fori_loop(0, nb, seq_body, 0, unroll=SAMPLE_SEQ_UNROLL)

    pairs = [(t, s) for t in range(steps) for s in range(t + 1)]
    dfac = [jnp.exp((b[t] - m_t[t]) + a[s]) for (t, s) in pairs]
    enm = [jnp.exp(-m_t[t]) for t in range(steps)]
    for hd in range(HEADS):
        hcols = slice(hd * DQK, (hd + 1) * DQK)
        vcols = slice(hd * DV, (hd + 1) * DV)
        qs = [slab(q_ref.at[hd], t) for t in range(steps)]
        ks = [slab(k_ref.at[hd], t) for t in range(steps)]
        vs = [slab_v(v_ref, hd, t) for t in range(steps)]
        n_old = n_ref[:, hcols]
        n_new = decay[:, hcols] * n_old
        for t in range(steps):
            n_new = n_new + w_in[t][:, hcols] * ks[t]
        nn_ref[:, hcols] = n_new
        prods = jnp.concatenate([qs[t] * ks[s] for (t, s) in pairs] + [qs[t] * n_old for t in range(steps)], axis=0)
        red = jnp.sum(prods, axis=1, keepdims=True)
        wts = red[:len(pairs) * nb] * jnp.concatenate([d[:, hcols] for d in dfac], axis=0)
        for t in range(steps):
            inter_t = inter[t][:, hcols]
            num = twice(inter_t) * slab_v(cq_scr, hd, t)
            den = inter_t * red[(len(pairs) + t) * nb:(len(pairs) + t + 1) * nb]
            for s in range(t + 1):
                p = pairs.index((t, s))
                w_ts = wts[p * nb:(p + 1) * nb]
                num = num + twice(w_ts) * vs[s]
                den = den + w_ts
            hout = num / twice(jnp.maximum(jnp.abs(den), enm[t][:, hcols]))
            hc_ref[t * nb:(t + 1) * nb, vcols] = _rms(hout, normg_ref[:, vcols])


def _mix_s3_kernel(x_ref, outa_ref, outb_ref, hc_ref, gpre_ref, wcat_ref, wg_ref, pa_ref, pb_ref, pc_ref,
                   wout_ref, gpost_ref, o_ref):
    x = x_ref[...]
    h, h_pieces = _prenorm(x, gpre_ref[...])
    o_ref[...] = _merge(x, h, h_pieces, hc_ref[...], wcat_ref, wg_ref, pa_ref, pb_ref, pc_ref, wout_ref, gpost_ref,
                        outa_ref, outb_ref)


def _mix_sample(xs, hist, c_all, c_buf, n_in, m_in, lw, layer):
    rows = xs.shape[0]
    nblk = rows // BLOCK_ROWS
    hist_rows = POOL_HIST * SAMPLE_SEQS_PER_BLOCK
    cs1 = functools.partial(_const_spec, index=layer, ngrid=1)
    full = lambda shape: pl.BlockSpec(shape, lambda i: (0,) * len(shape))
    params = pltpu.CompilerParams(dimension_semantics=("arbitrary",), vmem_limit_bytes=VMEM_LIMIT_BYTES)

    w1 = [lw['mix_norm_pre'], lw['wcat'], lw['wifc'], lw['ln_g'], lw['ln_b'], lw['mixm_s'], lw['mixb_s'],
          lw['pool_w'], lw['pool_scale']]
    s1_shapes = (
        jax.ShapeDtypeStruct((rows, 512), BF16),
        jax.ShapeDtypeStruct((rows, 512), BF16),
        jax.ShapeDtypeStruct((HEADS, rows, DQK), F32),
        jax.ShapeDtypeStruct((HEADS, rows, DQK), F32),
        jax.ShapeDtypeStruct((D_MODEL // 128, rows, 128), F32),
        jax.ShapeDtypeStruct((rows, HEADS * DQK), F32),
        jax.ShapeDtypeStruct((rows, HEADS * DQK), F32),
        jax.ShapeDtypeStruct((rows, 512), F32),
        jax.ShapeDtypeStruct((nblk, hist_rows, 512), F32),
    )
    outa, outb, q, k, v, gi, gf, vn, pool_new = pl.pallas_call(
        _mix_s1_kernel,
        grid=(1,),
        in_specs=[full(xs.shape), full(hist.shape)] + [cs1(w.shape) for w in w1],
        out_specs=tuple(full(s.shape) for s in s1_shapes),
        out_shape=s1_shapes,
        scratch_shapes=[pltpu.VMEM((rows, 512), BF16)],
        compiler_params=params,
        name="mix_s1",
    )(xs, hist, *w1)

    nb = SAMPLE_SEQS_PER_BLOCK
    blk = lambda width: pl.BlockSpec((BLOCK_ROWS, width), lambda j: (j, 0))
    seqblk = lambda width: pl.BlockSpec((nb, width), lambda j: (j, 0))
    cblk = pl.BlockSpec((None, nb, HEADS, DV, DQK), lambda j: (layer, j, 0, 0, 0))
    lrow = lambda width: pl.BlockSpec((None, 1, width), lambda j: (layer, 0, 0))
    colblk = lambda n: pl.BlockSpec((n, BLOCK_ROWS, 128), lambda j: (0, j, 0))
    hc, c_buf, n_new, m_new = pl.pallas_call(
        _mix_s2_kernel,
        grid=(nblk,),
        in_specs=[colblk(HEADS), colblk(HEADS), colblk(D_MODEL // 128), blk(512), blk(512),
                  lrow(512), lrow(512), lrow(D_MODEL), cblk, seqblk(512), seqblk(512),
                  pl.BlockSpec(memory_space=pl.ANY)],
        out_specs=(blk(D_MODEL), cblk, seqblk(512), seqblk(512)),
        out_shape=(jax.ShapeDtypeStruct((rows, D_MODEL), F32),
                   jax.ShapeDtypeStruct(c_buf.shape, F32),
                   jax.ShapeDtypeStruct(n_in.shape, F32),
                   jax.ShapeDtypeStruct(m_in.shape, F32)),
        input_output_aliases={11: 1},
        scratch_shapes=[pltpu.VMEM((D_MODEL // 128, BLOCK_ROWS, 128), F32),
                        pltpu.VMEM((HEADS, BLOCK_ROWS, 128), F32), pltpu.VMEM((HEADS, BLOCK_ROWS, 128), F32)],
        compiler_params=pltpu.CompilerParams(dimension_semantics=("parallel",),
                                             vmem_limit_bytes=VMEM_LIMIT_BYTES),
        name="mix_s2",
    )(q, k, v, gi, gf, lw['ib'], lw['fb'], lw['norm_g'], c_all, n_in, m_in, c_buf)

    w3 = [lw['mix_norm_pre'], lw['wcat'], lw['wg'], lw['proj_a'], lw['proj_b'], lw['proj_c'], lw['w_out'],
          lw['mix_norm_post']]
    y = pl.pallas_call(
        _mix_s3_kernel,
        grid=(1,),
        in_specs=[full(xs.shape), full(outa.shape), full(outb.shape), full(hc.shape)]
        + [cs1(w.shape) for w in w3],
        out_specs=full(xs.shape),
        out_shape=jax.ShapeDtypeStruct(xs.shape, F32),
        compiler_params=params,
        name="mix_s3",
    )(xs, outa, outb, hc, *w3)
    return y, vn, pool_new, c_buf, n_new, m_new


def _to_blocked(x):
    nseq, t, c = x.shape
    nb = SAMPLE_SEQS_PER_BLOCK
    return x.reshape(nseq // nb, nb, t, c).transpose(0, 2, 1, 3).reshape(nseq * t, c)


def _from_blocked(x, t):
    rows, c = x.shape
    nb = SAMPLE_SEQS_PER_BLOCK
    nseq = rows // t
    return x.reshape(nseq // nb, t, nb, c).transpose(0, 2, 1, 3).reshape(nseq, t, c)


def _prep_weights(p):
    w_in = p['w_in']
    row = lambda a: a.reshape(DEPTH, 1, -1).astype(F32)
    wi = w_in[:, :, 4608:4612]
    wf = w_in[:, :, 4612:4616]
    wift = jnp.zeros((DEPTH, 32, D_MODEL), F32)
    wift = wift.at[:, 0:HEADS].set(wi.transpose(0, 2, 1)).at[:, 16:16 + HEADS].set(wf.transpose(0, 2, 1))
    wifc = jnp.concatenate([jnp.repeat(wi, DQK, axis=2), jnp.repeat(wf, DQK, axis=2)], axis=2)
    gbias = jnp.zeros((DEPTH, 32, BLOCK_ROWS), F32)
    gbias = gbias.at[:, 0:HEADS].set(jnp.broadcast_to(p['mlstm_i_bias'][:, :, None], (DEPTH, HEADS, BLOCK_ROWS)))
    gbias = gbias.at[:, 16:16 + HEADS].set(
        jnp.broadcast_to(p['mlstm_f_bias'][:, :, None], (DEPTH, HEADS, BLOCK_ROWS)))
    lane_pad = lambda a: jnp.repeat(a.astype(F32), DQK, axis=1).reshape(DEPTH, 1, HEADS * DQK)

    ws = p['gmlp_w_s']
    bs = p['gmlp_b_s']
    mixm_p = jnp.tril(ws)
    mixb_p = jnp.broadcast_to(bs[..., None], ws.shape)
    nb = SAMPLE_SEQS_PER_BLOCK
    tril_s = jnp.tril(ws[:, :, :SAMPLE_STEPS, :SAMPLE_STEPS])
    eye = jnp.eye(nb, dtype=F32)
    mixm_s = jnp.einsum('lgts,bc->lgtbsc', tril_s, eye).reshape(DEPTH, N_GROUPS, BLOCK_ROWS, BLOCK_ROWS)
    mixb_s = jnp.broadcast_to(jnp.repeat(bs[:, :, :SAMPLE_STEPS], nb, axis=2)[..., None], ws.shape)

    return dict(
        ffn1_norm_pre=row(p['ffn1_norm_pre']), ffn1_w_in=p['ffn1_w_in'],
        ffn1_w_down=p['ffn1_w_down'], ffn1_norm_post=row(p['ffn1_norm_post']),
        ffn2_norm_pre=row(p['ffn2_norm_pre']), ffn2_w_in=p['ffn2_w_in'],
        ffn2_w_down=p['ffn2_w_down'], ffn2_norm_post=row(p['ffn2_norm_post']),
        mix_norm_pre=row(p['mix_norm_pre']), mix_norm_post=row(p['mix_norm_post']),
        wcat=w_in[:, :, :WCAT_COLS].astype(BF16), wg=w_in[:, :, WCAT_COLS + GATE_COLS:].astype(BF16),
        wift=wift.astype(BF16), wifc=wifc.astype(BF16), gbias=gbias,
        ib=lane_pad(p['mlstm_i_bias']), fb=lane_pad(p['mlstm_f_bias']),
        ln_g=row(p['gmlp_ln_g']), ln_b=row(p['gmlp_ln_b']),
        mixm_p=mixm_p.astype(BF16), mixb_p=mixb_p.astype(F32),
        mixm_s=mixm_s.astype(BF16), mixb_s=mixb_s.astype(F32),
        pool_w=p['pool_w'].astype(BF16), pool_scale=row(p['pool_scale']), norm_g=row(p['mlstm_norm_g']),
        proj_a=p['proj_a'].astype(BF16), proj_b=p['proj_b'].astype(BF16), proj_c=p['proj_c'].astype(BF16),
        w_out=p['w_out'].astype(BF16),
    )


def kernel(x_prompt, x_sample, state_pool, state_mlstm_C, state_mlstm_n, state_mlstm_m, ffn1_norm_pre, ffn1_w_in, ffn1_w_down, ffn1_norm_post, mix_norm_pre, w_in, gmlp_ln_g, gmlp_ln_b, gmlp_w_s, gmlp_b_s, pool_w, pool_scale, mlstm_i_bias, mlstm_f_bias, mlstm_norm_g, proj_a, proj_b, proj_c, w_out, mix_norm_post, ffn2_norm_pre, ffn2_w_in, ffn2_w_down, ffn2_norm_post):
    params = dict(ffn1_norm_pre=ffn1_norm_pre, ffn1_w_in=ffn1_w_in, ffn1_w_down=ffn1_w_down,
                  ffn1_norm_post=ffn1_norm_post, mix_norm_pre=mix_norm_pre, w_in=w_in, gmlp_ln_g=gmlp_ln_g,
                  gmlp_ln_b=gmlp_ln_b, gmlp_w_s=gmlp_w_s, gmlp_b_s=gmlp_b_s, pool_w=pool_w,
                  pool_scale=pool_scale, mlstm_i_bias=mlstm_i_bias, mlstm_f_bias=mlstm_f_bias,
                  mlstm_norm_g=mlstm_norm_g, proj_a=proj_a, proj_b=proj_b, proj_c=proj_c, w_out=w_out,
                  mix_norm_post=mix_norm_post, ffn2_norm_pre=ffn2_norm_pre, ffn2_w_in=ffn2_w_in,
                  ffn2_w_down=ffn2_w_down, ffn2_norm_post=ffn2_norm_post)
    lw = _prep_weights(params)
    nbp, seq, _ = x_prompt.shape
    nbs, steps, _ = x_sample.shape

    yp = x_prompt
    ys = _to_blocked(x_sample)
    hist_all = jnp.pad(state_pool, ((0, 0), (0, 0), (1, 0), (0, 0)))
    n_all = state_mlstm_n.reshape(DEPTH, nbs, HEADS * DQK)
    m_all = jnp.repeat(state_mlstm_m, DQK, axis=2)

    pool_p, c_p, n_p, m_p = [], [], [], []
    pool_s, n_s, m_s, v_s = [], [], [], []
    c_sample = jnp.zeros(state_mlstm_C.shape, F32)
    for l in range(DEPTH):
        ffn1 = (lw['ffn1_norm_pre'], lw['ffn1_w_in'], lw['ffn1_w_down'], lw['ffn1_norm_post'])
        ffn2 = (lw['ffn2_norm_pre'], lw['ffn2_w_in'], lw['ffn2_w_down'], lw['ffn2_norm_post'])
        yp = _ffn(yp.reshape(nbp * seq, D_MODEL), *ffn1, l).reshape(nbp, seq, D_MODEL)
        yp, pb, cf, nf_, mf = _mix_prompt(yp, lw, l)
        yp = _ffn(yp.reshape(nbp * seq, D_MODEL), *ffn2, l).reshape(nbp, seq, D_MODEL)
        pool_p.append(pb[:, 1:, :])
        c_p.append(cf)
        n_p.append(nf_[:, :HEADS, :])
        m_p.append(mf[:, :HEADS, 0])
        ys = _ffn(ys, *ffn1, l)
        hist = _to_blocked(hist_all[l]).reshape(nbs // SAMPLE_SEQS_PER_BLOCK,
                                                POOL_HIST * SAMPLE_SEQS_PER_BLOCK, 512)
        ys, vn, pool_new, c_sample, n_new, m_new = _mix_sample(
            ys, hist, state_mlstm_C, c_sample, n_all[l], m_all[l], lw, l)
        ys = _ffn(ys, *ffn2, l)
        pool_s.append(_from_blocked(pool_new.reshape(nbs * POOL_HIST, 512), POOL_HIST)[:, 1:, :])
        n_s.append(n_new.reshape(nbs, HEADS, DQK))
        m_s.append(m_new.reshape(nbs, HEADS, DQK)[:, :, 0])
        v_s.append(_from_blocked(vn, steps))
    return (yp, _from_blocked(ys, steps),
            jnp.stack(pool_p), jnp.stack(c_p), jnp.stack(n_p), jnp.stack(m_p),
            jnp.stack(pool_s), c_sample, jnp.stack(n_s), jnp.stack(m_s), jnp.stack(v_s))
```

```python
import functools

import jax
import jax.numpy as jnp
from jax import lax
from jax.experimental import pallas as pl
from jax.experimental.pallas import tpu as pltpu

F32 = jnp.float32
BF16 = jnp.bfloat16

D_MODEL = 1024
D_FF = 2816
DEPTH = 4
EPS = 1e-6
PAST_LEN = 16384
N_GROUPS = 4
GROUP_W = 128
POOL_WINDOWS = (2, 4, 8, 16)
POOL_HIST = 16
HEADS = 4
DQK = 128
DV = 256
BLOCK_ROWS = 128
SAMPLE_SEQS_PER_BLOCK = 16
SAMPLE_STEPS = 8

OFF_AU, OFF_AV, OFF_B, OFF_Q, OFF_K, OFF_V, OFF_O = 0, 512, 1024, 1536, 2048, 2560, 3584
WCAT_COLS = 4608
GATE_COLS = 8

SAMPLE_SEQ_UNROLL = 4

FFN_TM = 2048
FFN_TF = 256
MIX_ROW_SPLIT = 2
FFN_ROW_BLOCKS = 4
PROMPT_SEQS_PER_STEP = 4
VMEM_LIMIT_BYTES = 56 * 1024 * 1024


def _dot(a, b):
    return jnp.dot(a, b, preferred_element_type=F32)


def _dot_nt(a, b):
    return lax.dot_general(a, b, (((1,), (1,)), ((), ())), preferred_element_type=F32)


def _dot_tn(a, b):
    return lax.dot_general(a, b, (((0,), (0,)), ((), ())), preferred_element_type=F32)


def _rms(x, g):
    return x * lax.rsqrt(jnp.mean(x * x, axis=-1, keepdims=True) + EPS) * g


def _sigmoid(x):
    return 1.0 / (1.0 + jnp.exp(-x))


def _gelu(x):
    return 0.5 * x * (1.0 + jnp.tanh(0.7978845608028654 * (x + 0.044715 * (x * x * x))))


def _log_sigmoid(x):
    return jnp.minimum(x, 0.0) - jnp.log1p(jnp.exp(-jnp.abs(x)))


def _const_spec(shape, index, ngrid):
    zeros = (0,) * (len(shape) - 1)
    if ngrid == 1:
        imap = lambda i: (index,) + zeros
    else:
        imap = lambda i, j: (index,) + zeros
    return pl.BlockSpec((None,) + tuple(shape[1:]), imap, pipeline_mode=pl.Buffered(1))


def _ffn_kernel(x_ref, gpre_ref, wgate_ref, wup_ref, wdown_ref, gpost_ref, o_ref, h_scr):
    f = pl.program_id(1)
    last = pl.num_programs(1) - 1
    tm = x_ref.shape[0]
    rb = tm // FFN_ROW_BLOCKS

    def step(first, final):
        wgate = wgate_ref[...]
        wup = wup_ref[...]
        wdown = wdown_ref[...]
        for r in range(FFN_ROW_BLOCKS):
            rows = slice(r * rb, (r + 1) * rb)
            if first:
                h = _rms(x_ref[rows, :], gpre_ref[...]).astype(BF16)
                h_scr[rows, :] = h
            else:
                h = h_scr[rows, :]
            gate = _dot(h, wgate)
            up = _dot(h, wup)
            act = (gate * _sigmoid(gate) * up).astype(BF16)
            part = _dot(act, wdown)
            acc = part if first else o_ref[rows, :] + part
            if final:
                acc = x_ref[rows, :] + _rms(acc, 0.5 * gpost_ref[...])
            o_ref[rows, :] = acc

    @pl.when(f == 0)
    def _():
        step(True, False)

    @pl.when(jnp.logical_and(f > 0, f < last))
    def _():
        step(False, False)

    @pl.when(f == last)
    def _():
        step(False, True)


def _ffn(x2d, gpre, w_in, wdown, gpost, layer):
    m = x2d.shape[0]
    tm = min(FFN_TM, m)
    nf = D_FF // FFN_TF
    return pl.pallas_call(
        _ffn_kernel,
        grid=(m // tm, nf),
        in_specs=[
            pl.BlockSpec((tm, D_MODEL), lambda i, f: (i, 0)),
            pl.BlockSpec((None, 1, D_MODEL), lambda i, f: (layer, 0, 0)),
            pl.BlockSpec((None, D_MODEL, FFN_TF), lambda i, f: (layer, 0, f)),
            pl.BlockSpec((None, D_MODEL, FFN_TF), lambda i, f: (layer, 0, f + nf)),
            pl.BlockSpec((None, FFN_TF, D_MODEL), lambda i, f: (layer, f, 0)),
            pl.BlockSpec((None, 1, D_MODEL), lambda i, f: (layer, 0, 0)),
        ],
        out_specs=pl.BlockSpec((tm, D_MODEL), lambda i, f: (i, 0)),
        out_shape=jax.ShapeDtypeStruct((m, D_MODEL), F32),
        scratch_shapes=[pltpu.VMEM((tm, D_MODEL), BF16)],
        compiler_params=pltpu.CompilerParams(
            dimension_semantics=("parallel", "arbitrary"), vmem_limit_bytes=VMEM_LIMIT_BYTES),
        name="ffn",
    )(x2d, gpre, w_in, w_in, wdown, gpost)


def _prenorm(x, g):
    step = x.shape[0] // MIX_ROW_SPLIT
    pieces = [_rms(x[i * step:(i + 1) * step, :], g).astype(BF16) for i in range(MIX_ROW_SPLIT)]
    return jnp.concatenate(pieces, axis=0), pieces


def _rows_dot(pieces, w):
    return jnp.concatenate([_dot(p, w) for p in pieces], axis=0)


def _branch_a(h, h_pieces, wcat_ref, lng_ref, lnb_ref, mixm_ref, mixb_ref, outa_scr, vn_out_ref=None):
    rows = h.shape[0]
    a_v = _gelu(_rows_dot(h_pieces, wcat_ref[:, OFF_AV:OFF_AV + 512]))
    mu = jnp.mean(a_v, axis=-1, keepdims=True)
    xc = a_v - mu
    var = jnp.mean(xc * xc, axis=-1, keepdims=True)
    v_n = xc * lax.rsqrt(var + EPS) * lng_ref[...] + lnb_ref[...]
    if vn_out_ref is not None:
        vn_out_ref[...] = v_n
    vb = v_n.astype(BF16)
    a_u = _gelu(_dot(h, wcat_ref[:, OFF_AU:OFF_AU + 512]))
    for g in range(N_GROUPS):
        cols = slice(g * GROUP_W, (g + 1) * GROUP_W)
        mix_g = mixm_ref[g]
        bias_g = mixb_ref[g]
        for s in range(rows // BLOCK_ROWS):
            rws = slice(s * BLOCK_ROWS, (s + 1) * BLOCK_ROWS)
            mixed = _dot(mix_g, vb[rws, cols]) + bias_g
            outa_scr[rws, cols] = (a_u[rws, cols] * mixed).astype(BF16)


def _branch_b(h, wcat_ref, poolw_ref, pscale_ref, hist_in, hist_out, pooled_scr, outb_scr, *, rs, pos1):
    rows = h.shape[0]
    hist_rows = POOL_HIST * rs
    b_in = _dot(h, wcat_ref[:, OFF_B:OFF_B + 512])
    for s in range(rows // BLOCK_ROWS):
        rws = slice(s * BLOCK_ROWS, (s + 1) * BLOCK_ROWS)
        cur = b_in[rws, :]
        ext = jnp.concatenate([hist_in(s), cur], axis=0)
        for g, w in enumerate(POOL_WINDOWS):
            cols = slice(g * GROUP_W, (g + 1) * GROUP_W)
            acc = ext[:, cols]
            sh = 1
            while sh < w:
                acc = acc + pltpu.roll(acc, sh * rs, axis=0)
                sh *= 2
            win = acc[hist_rows:, :]
            if pos1 is None:
                mean = win * (1.0 / w)
            else:
                mean = win / jnp.minimum(pos1, w).astype(F32)
            pooled_scr[rws, cols] = (mean - cur[:, cols]).astype(BF16)
        hist_out(s, ext[BLOCK_ROWS:, :])
    for g in range(N_GROUPS):
        cols = slice(g * GROUP_W, (g + 1) * GROUP_W)
        mixed = _dot(pooled_scr[:, cols], poolw_ref[g]) * pscale_ref[:, cols]
        outb_scr[:, cols] = mixed.astype(BF16)


def _merge(x, h, h_pieces, hc, wcat_ref, wg_ref, pa_ref, pb_ref, pc_ref, wout_ref, gpost_ref, outa_scr, outb_scr):
    merged = _sigmoid(_rows_dot(h_pieces, wg_ref[:, 0:D_MODEL])) * _dot(outa_scr[...], pa_ref[...])
    merged += _sigmoid(_dot(h, wg_ref[:, D_MODEL:2 * D_MODEL])) * _dot(outb_scr[...], pb_ref[...])
    out_c = (_sigmoid(_dot(h, wcat_ref[:, OFF_O:OFF_O + D_MODEL])) * hc).astype(BF16)
    merged += _sigmoid(_dot(h, wg_ref[:, 2 * D_MODEL:3 * D_MODEL])) * _dot(out_c, pc_ref[...])
    mb = merged.astype(BF16)
    step = x.shape[0] // MIX_ROW_SPLIT
    outs = []
    for i in range(MIX_ROW_SPLIT):
        rws = slice(i * step, (i + 1) * step)
        outs.append(x[rws, :] + _rms(_dot(mb[rws, :], wout_ref[...]), gpost_ref[...]))
    return jnp.concatenate(outs, axis=0)


def _scan_lanes(x, op, fill):
    lane = lax.broadcasted_iota(jnp.int32, x.shape, 1)
    sh = 1
    while sh < x.shape[1]:
        x = op(x, jnp.where(lane >= sh, pltpu.roll(x, sh, axis=1), fill))
        sh *= 2
    return x


def _last_lane(x):
    lane = lax.broadcasted_iota(jnp.int32, x.shape, 1)
    return jnp.max(jnp.where(lane == x.shape[1] - 1, x, -jnp.inf), axis=1, keepdims=True)


def _mlstm_gates(h_pieces, wift_ref, gbias_ref, m_ref, *, nseq):
    t_len = BLOCK_ROWS
    grow = jnp.concatenate([_dot_nt(wift_ref[...], p) for p in h_pieces], axis=1)
    seqs = range(nseq)
    ig = jnp.concatenate([grow[0:16, s * t_len:(s + 1) * t_len] + gbias_ref[0:16, :] for s in seqs], axis=0)
    lf = _log_sigmoid(
        jnp.concatenate([grow[16:32, s * t_len:(s + 1) * t_len] + gbias_ref[16:32, :] for s in seqs], axis=0))
    m0 = m_ref[...].reshape(nseq * 16, t_len)
    b = _scan_lanes(lf, jnp.add, 0.0)
    a = ig - b
    m_t = b + jnp.maximum(m0, _scan_lanes(a, jnp.maximum, -jnp.inf))
    inter = jnp.exp(b + m0 - m_t)
    b_last = _last_lane(b)
    m_end = _last_lane(m_t)
    decay = jnp.exp(b_last + m0 - m_end)
    w_in = jnp.exp(a + b_last - m_end)
    m_ref[...] = jnp.broadcast_to(m_end, (nseq * 16, t_len)).reshape(nseq, 16, t_len)
    stacked = jnp.concatenate([b - m_t, inter, jnp.exp(-m_t), w_in], axis=0)
    pad = (-stacked.shape[0]) % t_len
    if pad:
        stacked = jnp.concatenate([stacked, jnp.zeros((pad, t_len), F32)], axis=0)
    cols_t = [stacked[i * t_len:(i + 1) * t_len, :].T for i in range(stacked.shape[0] // t_len)]

    def col(quantity, s, hd):
        idx = quantity * nseq * 16 + s * 16 + hd
        return cols_t[idx // t_len][:, idx % t_len:idx % t_len + 1]

    return a, decay, col


def _mlstm_prompt(h, gates, wcat_ref, normg_ref, c_ref, n_ref, hc_scr, *, nseq):
    t_len = BLOCK_ROWS
    a_all, decay_all, col = gates
    q = _dot(h, wcat_ref[:, OFF_Q:OFF_Q + 512])
    k = _dot(h, wcat_ref[:, OFF_K:OFF_K + 512]) * (DQK ** -0.5)
    v = _dot(h, wcat_ref[:, OFF_V:OFF_V + D_MODEL])
    ri = lax.broadcasted_iota(jnp.int32, (t_len, t_len), 0)
    ci = lax.broadcasted_iota(jnp.int32, (t_len, t_len), 1)
    causal = ci <= ri
    for s in range(nseq):
        rws = slice(s * t_len, (s + 1) * t_len)
        a = a_all[s * 16:(s + 1) * 16, :]
        decay = decay_all[s * 16:(s + 1) * 16, :]
        for hd in range(HEADS):
            qh = q[rws, hd * DQK:(hd + 1) * DQK]
            kh = k[rws, hd * DQK:(hd + 1) * DQK]
            vh = v[rws, hd * DV:(hd + 1) * DV]
            qb = qh.astype(BF16)
            kb = kh.astype(BF16)
            bm_c = col(0, s, hd)
            inter_c = col(1, s, hd)
            en_c = col(2, s, hd)
            win_c = col(3, s, hd)
            dmat = jnp.exp(jnp.where(causal, bm_c + a[hd:hd + 1, :], -jnp.inf))
            wmat = _dot_nt(qb, kb) * dmat
            c_old = c_ref[s, hd]
            n_old = n_ref[s, hd:hd + 1, :]
            num = _dot(wmat.astype(BF16), vh.astype(BF16)) + inter_c * _dot_nt(qb, c_old.astype(BF16))
            den = jnp.sum(wmat, axis=1, keepdims=True) + inter_c * jnp.sum(qh * n_old, axis=1, keepdims=True)
            hout = num / jnp.maximum(jnp.abs(den), en_c)
            hc_scr[rws, hd * DV:(hd + 1) * DV] = _rms(hout, normg_ref[:, hd * DV:(hd + 1) * DV])
            dec_row = decay[hd:hd + 1, :]
            c_ref[s, hd] = dec_row * c_old + _dot_tn((vh * win_c).astype(BF16), kb)
            n_ref[s, hd:hd + 1, :] = dec_row * n_old + jnp.sum(kh * win_c, axis=0, keepdims=True)


def _mix_prompt_kernel(x_ref, gpre_ref, wcat_ref, wg_ref, wift_ref, gbias_ref, lng_ref, lnb_ref, mixm_ref, mixb_ref,
                       poolw_ref, pscale_ref, normg_ref, pa_ref, pb_ref, pc_ref, wout_ref, gpost_ref,
                       o_ref, pool_ref, c_ref, n_ref, m_ref,
                       outa_scr, outb_scr, pooled_scr, hc_scr, *, nseq):
    c = pl.program_id(1)
    rows = nseq * BLOCK_ROWS

    @pl.when(c == 0)
    def _():
        pool_ref[...] = jnp.zeros(pool_ref.shape, F32)
        c_ref[...] = jnp.zeros(c_ref.shape, F32)
        n_ref[...] = jnp.zeros(n_ref.shape, F32)
        m_ref[...] = jnp.zeros(m_ref.shape, F32)

    x = x_ref[...].reshape(rows, D_MODEL)
    h, h_pieces = _prenorm(x, gpre_ref[...])

    gates = _mlstm_gates(h_pieces, wift_ref, gbias_ref, m_ref, nseq=nseq)

    _branch_a(h, h_pieces, wcat_ref, lng_ref, lnb_ref, mixm_ref, mixb_ref, outa_scr)

    pos1 = c * BLOCK_ROWS + 1 + lax.broadcasted_iota(jnp.int32, (BLOCK_ROWS, GROUP_W), 0)

    def hist_out(s, val):
        pool_ref[s] = val

    _branch_b(h, wcat_ref, poolw_ref, pscale_ref, lambda s: pool_ref[s], hist_out, pooled_scr, outb_scr,
              rs=1, pos1=pos1)

    _mlstm_prompt(h, gates, wcat_ref, normg_ref, c_ref, n_ref, hc_scr, nseq=nseq)

    y = _merge(x, h, [h], hc_scr[...], wcat_ref, wg_ref, pa_ref, pb_ref, pc_ref, wout_ref, gpost_ref,
               outa_scr, outb_scr)
    o_ref[...] = y.reshape(nseq, BLOCK_ROWS, D_MODEL)


def _mix_prompt(x3d, lw, layer):
    nb, seq, _ = x3d.shape
    nseq = PROMPT_SEQS_PER_STEP
    rows = nseq * BLOCK_ROWS
    cs = functools.partial(_const_spec, index=layer, ngrid=2)
    weights = [lw['mix_norm_pre'], lw['wcat'], lw['wg'], lw['wift'], lw['gbias'], lw['ln_g'], lw['ln_b'],
               lw['mixm_p'], lw['mixb_p'], lw['pool_w'], lw['pool_scale'], lw['norm_g'],
               lw['proj_a'], lw['proj_b'], lw['proj_c'], lw['w_out'], lw['mix_norm_post']]
    out_shapes = (
        jax.ShapeDtypeStruct((nb, seq, D_MODEL), F32),
        jax.ShapeDtypeStruct((nb, POOL_HIST, 512), F32),
        jax.ShapeDtypeStruct((nb, HEADS, DV, DQK), F32),
        jax.ShapeDtypeStruct((nb, 8, DQK), F32),
        jax.ShapeDtypeStruct((nb, 16, BLOCK_ROWS), F32),
    )
    out_specs = (
        pl.BlockSpec((nseq, BLOCK_ROWS, D_MODEL), lambda g, c: (g, c, 0)),
        pl.BlockSpec((nseq, POOL_HIST, 512), lambda g, c: (g, 0, 0)),
        pl.BlockSpec((nseq, HEADS, DV, DQK), lambda g, c: (g, 0, 0, 0)),
        pl.BlockSpec((nseq, 8, DQK), lambda g, c: (g, 0, 0)),
        pl.BlockSpec((nseq, 16, BLOCK_ROWS), lambda g, c: (g, 0, 0)),
    )
    return pl.pallas_call(
        functools.partial(_mix_prompt_kernel, nseq=nseq),
        grid=(nb // nseq, seq // BLOCK_ROWS),
        in_specs=[pl.BlockSpec((nseq, BLOCK_ROWS, D_MODEL), lambda g, c: (g, c, 0))]
        + [cs(w.shape) for w in weights],
        out_specs=out_specs,
        out_shape=out_shapes,
        scratch_shapes=[pltpu.VMEM((rows, 512), BF16), pltpu.VMEM((rows, 512), BF16),
                        pltpu.VMEM((rows, 512), BF16), pltpu.VMEM((rows, D_MODEL), F32)],
        compiler_params=pltpu.CompilerParams(
            dimension_semantics=("parallel", "arbitrary"), vmem_limit_bytes=VMEM_LIMIT_BYTES),
        name="mix_prompt",
    )(x3d, *weights)


def _mix_s1_kernel(x_ref, hist_ref, gpre_ref, wcat_ref, wifc_ref, lng_ref, lnb_ref, mixm_ref, mixb_ref,
                   poolw_ref, pscale_ref,
                   outa_ref, outb_ref, q_ref, k_ref, v_ref, gi_ref, gf_ref, vn_ref, pool_ref,
                   pooled_scr):
    h, h_pieces = _prenorm(x_ref[...], gpre_ref[...])
    _branch_a(h, h_pieces, wcat_ref, lng_ref, lnb_ref, mixm_ref, mixb_ref, outa_ref, vn_out_ref=vn_ref)

    def hist_out(s, val):
        pool_ref[s] = val

    _branch_b(h, wcat_ref, poolw_ref, pscale_ref, lambda s: hist_ref[s], hist_out, pooled_scr, outb_ref,
              rs=SAMPLE_SEQS_PER_BLOCK, pos1=None)
    q = _dot(h, wcat_ref[:, OFF_Q:OFF_Q + 512])
    k = _dot(h, wcat_ref[:, OFF_K:OFF_K + 512]) * (DQK ** -0.5)
    v = _dot(h, wcat_ref[:, OFF_V:OFF_V + D_MODEL])
    for j in range(512 // 128):
        q_ref[j] = q[:, j * 128:(j + 1) * 128]
        k_ref[j] = k[:, j * 128:(j + 1) * 128]
    for j in range(D_MODEL // 128):
        v_ref[j] = v[:, j * 128:(j + 1) * 128]
    gi_ref[...] = _dot(h, wifc_ref[:, 0:HEADS * DQK])
    gf_ref[...] = _dot(h, wifc_ref[:, HEADS * DQK:2 * HEADS * DQK])


def _mix_s2_kernel(q_ref, k_ref, v_ref, gi_ref, gf_ref, ib_ref, fb_ref, normg_ref, c_ref, n_ref, m_ref,
                   cbuf_ref, hc_ref, cn_ref, nn_ref, mn_ref,
                   cq_scr, win_scr, dec_scr):
    del cbuf_ref
    nb = SAMPLE_SEQS_PER_BLOCK
    steps = SAMPLE_STEPS

    vblocks = DV // 128

    def slab(ref, t):
        return ref[t * nb:(t + 1) * nb, :]

    def slab_v(ref, hd, t):
        return jnp.concatenate([ref[hd * vblocks + j, t * nb:(t + 1) * nb, :] for j in range(vblocks)], axis=1)

    def twice(x):
        return jnp.concatenate([x] * vblocks, axis=1)

    m0 = m_ref[...]
    ig = [slab(gi_ref, t) + ib_ref[...] for t in range(steps)]
    lf = [_log_sigmoid(slab(gf_ref, t) + fb_ref[...]) for t in range(steps)]
    b, a, m_t = [], [], []
    run_b = jnp.zeros_like(m0)
    run_max = jnp.full(m0.shape, -jnp.inf, F32)
    for t in range(steps):
        run_b = run_b + lf[t]
        b.append(run_b)
        a.append(ig[t] - run_b)
        run_max = jnp.maximum(run_max, a[t])
        m_t.append(run_b + jnp.maximum(m0, run_max))
    inter = [jnp.exp(b[t] + m0 - m_t[t]) for t in range(steps)]
    m_end = m_t[-1]
    decay = jnp.exp(b[-1] + m0 - m_end)
    w_in = [jnp.exp(a[t] + b[-1] - m_end) for t in range(steps)]
    mn_ref[...] = m_end
    for hd in range(HEADS):
        hcols = slice(hd * DQK, (hd + 1) * DQK)
        for t in range(steps):
            win_scr[hd, t * nb:(t + 1) * nb, :] = w_in[t][:, hcols]
            dec_scr[hd, t * nb:(t + 1) * nb, :] = decay[:, hcols]

    def seq_body(bi, carry):
        rows = pl.ds(bi, steps, stride=nb)
        for hd in range(HEADS):
            c_old = c_ref[bi, hd]
            qb = q_ref[hd, rows, :].astype(BF16)
            cq = _dot_nt(qb, c_old.astype(BF16))
            for j in range(vblocks):
                cq_scr[hd * vblocks + j, rows, :] = cq[:, j * 128:(j + 1) * 128]
            vrows = jnp.concatenate([v_ref[hd * vblocks + j, rows, :] for j in range(vblocks)], axis=1)
            vs = (vrows * twice(win_scr[hd, rows, :])).astype(BF16)
            kb = k_ref[hd, rows, :].astype(BF16)
            dec_row = dec_scr[hd, rows, :][0:1, :]
            cn_ref[bi, hd] = dec_row * c_old + _dot_tn(vs, kb)
        return carry

    lax.fori_loop(0, nb, seq_body, 0, unroll=SAMPLE_SEQ_UNROLL)

    pairs = [(t, s) for t in range(steps) for s in range(t + 1)]
    dfac = [jnp.exp((b[t] - m_t[t]) + a[s]) for (t, s) in pairs]
    enm = [jnp.exp(-m_t[t]) for t in range(steps)]
    for hd in range(HEADS):
        hcols = slice(hd * DQK, (hd + 1) * DQK)
        vcols = slice(hd * DV, (hd + 1) * DV)
        qs = [slab(q_ref.at[hd], t) for t in range(steps)]
        ks = [slab(k_ref.at[hd], t) for t in range(steps)]
        vs = [slab_v(v_ref, hd, t) for t in range(steps)]
        n_old = n_ref[:, hcols]
        n_new = decay[:, hcols] * n_old
        for t in range(steps):
            n_new = n_new + w_in[t][:, hcols] * ks[t]
        nn_ref[:, hcols] = n_new
        prods = jnp.concatenate([qs[t] * ks[s] for (t, s) in pairs] + [qs[t] * n_old for t in range(steps)], axis=0)
        red = jnp.sum(prods, axis=1, keepdims=True)
        wts = red[:len(pairs) * nb] * jnp.concatenate([d[:, hcols] for d in dfac], axis=0)
        for t in range(steps):
            inter_t = inter[t][:, hcols]
            num = twice(inter_t) * slab_v(cq_scr, hd, t)
            den = inter_t * red[(len(pairs) + t) * nb:(len(pairs) + t + 1) * nb]
            for s in range(t + 1):
                p = pairs.index((t, s))
                w_ts = wts[p * nb:(p + 1) * nb]
                num = num + twice(w_ts) * vs[s]
                den = den + w_ts
            hout = num / twice(jnp.maximum(jnp.abs(den), enm[t][:, hcols]))
            hc_ref[t * nb:(t + 1) * nb, vcols] = _rms(hout, normg_ref[:, vcols])


def _mix_s3_kernel(x_ref, outa_ref, outb_ref, hc_ref, gpre_ref, wcat_ref, wg_ref, pa_ref, pb_ref, pc_ref,
                   wout_ref, gpost_ref, o_ref):
    x = x_ref[...]
    h, h_pieces = _prenorm(x, gpre_ref[...])
    o_ref[...] = _merge(x, h, h_pieces, hc_ref[...], wcat_ref, wg_ref, pa_ref, pb_ref, pc_ref, wout_ref, gpost_ref,
                        outa_ref, outb_ref)


def _mix_sample(xs, hist, c_all, c_buf, n_in, m_in, lw, layer):
    rows = xs.shape[0]
    nblk = rows // BLOCK_ROWS
    hist_rows = POOL_HIST * SAMPLE_SEQS_PER_BLOCK
    cs1 = functools.partial(_const_spec, index=layer, ngrid=1)
    full = lambda shape: pl.BlockSpec(shape, lambda i: (0,) * len(shape))
    params = pltpu.CompilerParams(dimension_semantics=("arbitrary",), vmem_limit_bytes=VMEM_LIMIT_BYTES)

    w1 = [lw['mix_norm_pre'], lw['wcat'], lw['wifc'], lw['ln_g'], lw['ln_b'], lw['mixm_s'], lw['mixb_s'],
          lw['pool_w'], lw['pool_scale']]
    s1_shapes = (
        jax.ShapeDtypeStruct((rows, 512), BF16),
        jax.ShapeDtypeStruct((rows, 512), BF16),
        jax.ShapeDtypeStruct((HEADS, rows, DQK), F32),
        jax.ShapeDtypeStruct((HEADS, rows, DQK), F32),
        jax.ShapeDtypeStruct((D_MODEL // 128, rows, 128), F32),
        jax.ShapeDtypeStruct((rows, HEADS * DQK), F32),
        jax.ShapeDtypeStruct((rows, HEADS * DQK), F32),
        jax.ShapeDtypeStruct((rows, 512), F32),
        jax.ShapeDtypeStruct((nblk, hist_rows, 512), F32),
    )
    outa, outb, q, k, v, gi, gf, vn, pool_new = pl.pallas_call(
        _mix_s1_kernel,
        grid=(1,),
        in_specs=[full(xs.shape), full(hist.shape)] + [cs1(w.shape) for w in w1],
        out_specs=tuple(full(s.shape) for s in s1_shapes),
        out_shape=s1_shapes,
        scratch_shapes=[pltpu.VMEM((rows, 512), BF16)],
        compiler_params=params,
        name="mix_s1",
    )(xs, hist, *w1)

    nb = SAMPLE_SEQS_PER_BLOCK
    blk = lambda width: pl.BlockSpec((BLOCK_ROWS, width), lambda j: (j, 0))
    seqblk = lambda width: pl.BlockSpec((nb, width), lambda j: (j, 0))
    cblk = pl.BlockSpec((None, nb, HEADS, DV, DQK), lambda j: (layer, j, 0, 0, 0))
    lrow = lambda width: pl.BlockSpec((None, 1, width), lambda j: (layer, 0, 0))
    colblk = lambda n: pl.BlockSpec((n, BLOCK_ROWS, 128), lambda j: (0, j, 0))
    hc, c_buf, n_new, m_new = pl.pallas_call(
        _mix_s2_kernel,
        grid=(nblk,),
        in_specs=[colblk(HEADS), colblk(HEADS), colblk(D_MODEL // 128), blk(512), blk(512),
                  lrow(512), lrow(512), lrow(D_MODEL), cblk, seqblk(512), seqblk(512),
                  pl.BlockSpec(memory_space=pl.ANY)],
        out_specs=(blk(D_MODEL), cblk, seqblk(512), seqblk(512)),
        out_shape=(jax.ShapeDtypeStruct((rows, D_MODEL), F32),
                   jax.ShapeDtypeStruct(c_buf.shape, F32),
                   jax.ShapeDtypeStruct(n_in.shape, F32),
                   jax.ShapeDtypeStruct(m_in.shape, F32)),
        input_output_aliases={11: 1},
        scratch_shapes=[pltpu.VMEM((D_MODEL // 128, BLOCK_ROWS, 128), F32),
                        pltpu.VMEM((HEADS, BLOCK_ROWS, 128), F32), pltpu.VMEM((HEADS, BLOCK_ROWS, 128), F32)],
        compiler_params=pltpu.CompilerParams(dimension_semantics=("parallel",),
                                             vmem_limit_bytes=VMEM_LIMIT_BYTES),
        name="mix_s2",
    )(q, k, v, gi, gf, lw['ib'], lw['fb'], lw['norm_g'], c_all, n_in, m_in, c_buf)

    w3 = [lw['mix_norm_pre'], lw['wcat'], lw['wg'], lw['proj_a'], lw['proj_b'], lw['proj_c'], lw['w_out'],
          lw['mix_norm_post']]
    y = pl.pallas_call(
        _mix_s3_kernel,
        grid=(1,),
        in_specs=[full(xs.shape), full(outa.shape), full(outb.shape), full(hc.shape)]
        + [cs1(w.shape) for w in w3],
        out_specs=full(xs.shape),
        out_shape=jax.ShapeDtypeStruct(xs.shape, F32),
        compiler_params=params,
        name="mix_s3",
    )(xs, outa, outb, hc, *w3)
    return y, vn, pool_new, c_buf, n_new, m_new


def _to_blocked(x):
    nseq, t, c = x.shape
    nb = SAMPLE_SEQS_PER_BLOCK
    return x.reshape(nseq // nb, nb, t, c).transpose(0, 2, 1, 3).reshape(nseq * t, c)


def _from_blocked(x, t):
    rows, c = x.shape
    nb = SAMPLE_SEQS_PER_BLOCK
    nseq = rows // t
    return x.reshape(nseq // nb, t, nb, c).transpose(0, 2, 1, 3).reshape(nseq, t, c)


def _prep_weights(p):
    w_in = p['w_in']
    row = lambda a: a.reshape(DEPTH, 1, -1).astype(F32)
    wi = w_in[:, :, 4608:4612]
    wf = w_in[:, :, 4612:4616]
    wift = jnp.zeros((DEPTH, 32, D_MODEL), F32)
    wift = wift.at[:, 0:HEADS].set(wi.transpose(0, 2, 1)).at[:, 16:16 + HEADS].set(wf.transpose(0, 2, 1))
    wifc = jnp.concatenate([jnp.repeat(wi, DQK, axis=2), jnp.repeat(wf, DQK, axis=2)], axis=2)
    gbias = jnp.zeros((DEPTH, 32, BLOCK_ROWS), F32)
    gbias = gbias.at[:, 0:HEADS].set(jnp.broadcast_to(p['mlstm_i_bias'][:, :, None], (DEPTH, HEADS, BLOCK_ROWS)))
    gbias = gbias.at[:, 16:16 + HEADS].set(
        jnp.broadcast_to(p['mlstm_f_bias'][:, :, None], (DEPTH, HEADS, BLOCK_ROWS)))
    lane_pad = lambda a: jnp.repeat(a.astype(F32), DQK, axis=1).reshape(DEPTH, 1, HEADS * DQK)

    ws = p['gmlp_w_s']
    bs = p['gmlp_b_s']
    mixm_p = jnp.tril(ws)
    mixb_p = jnp.broadcast_to(bs[..., None], ws.shape)
    nb = SAMPLE_SEQS_PER_BLOCK
    tril_s = jnp.tril(ws[:, :, :SAMPLE_STEPS, :SAMPLE_STEPS])
    eye = jnp.eye(nb, dtype=F32)
    mixm_s = jnp.einsum('lgts,bc->lgtbsc', tril_s, eye).reshape(DEPTH, N_GROUPS, BLOCK_ROWS, BLOCK_ROWS)
    mixb_s = jnp.broadcast_to(jnp.repeat(bs[:, :, :SAMPLE_STEPS], nb, axis=2)[..., None], ws.shape)

    return dict(
        ffn1_norm_pre=row(p['ffn1_norm_pre']), ffn1_w_in=p['ffn1_w_in'].astype(BF16),
        ffn1_w_down=p['ffn1_w_down'].astype(BF16), ffn1_norm_post=row(p['ffn1_norm_post']),
        ffn2_norm_pre=row(p['ffn2_norm_pre']), ffn2_w_in=p['ffn2_w_in'].astype(BF16),
        ffn2_w_down=p['ffn2_w_down'].astype(BF16), ffn2_norm_post=row(p['ffn2_norm_post']),
        mix_norm_pre=row(p['mix_norm_pre']), mix_norm_post=row(p['mix_norm_post']),
        wcat=w_in[:, :, :WCAT_COLS].astype(BF16), wg=w_in[:, :, WCAT_COLS + GATE_COLS:].astype(BF16),
        wift=wift.astype(BF16), wifc=wifc.astype(BF16), gbias=gbias,
        ib=lane_pad(p['mlstm_i_bias']), fb=lane_pad(p['mlstm_f_bias']),
        ln_g=row(p['gmlp_ln_g']), ln_b=row(p['gmlp_ln_b']),
        mixm_p=mixm_p.astype(BF16), mixb_p=mixb_p.astype(F32),
        mixm_s=mixm_s.astype(BF16), mixb_s=mixb_s.astype(F32),
        pool_w=p['pool_w'].astype(BF16), pool_scale=row(p['pool_scale']), norm_g=row(p['mlstm_norm_g']),
        proj_a=p['proj_a'].astype(BF16), proj_b=p['proj_b'].astype(BF16), proj_c=p['proj_c'].astype(BF16),
        w_out=p['w_out'].astype(BF16),
    )


def kernel(x_prompt, x_sample, state_pool, state_mlstm_C, state_mlstm_n, state_mlstm_m, ffn1_norm_pre, ffn1_w_in, ffn1_w_down, ffn1_norm_post, mix_norm_pre, w_in, gmlp_ln_g, gmlp_ln_b, gmlp_w_s, gmlp_b_s, pool_w, pool_scale, mlstm_i_bias, mlstm_f_bias, mlstm_norm_g, proj_a, proj_b, proj_c, w_out, mix_norm_post, ffn2_norm_pre, ffn2_w_in, ffn2_w_down, ffn2_norm_post):
    params = dict(ffn1_norm_pre=ffn1_norm_pre, ffn1_w_in=ffn1_w_in, ffn1_w_down=ffn1_w_down,
                  ffn1_norm_post=ffn1_norm_post, mix_norm_pre=mix_norm_pre, w_in=w_in, gmlp_ln_g=gmlp_ln_g,
                  gmlp_ln_b=gmlp_ln_b, gmlp_w_s=gmlp_w_s, gmlp_b_s=gmlp_b_s, pool_w=pool_w,
                  pool_scale=pool_scale, mlstm_i_bias=mlstm_i_bias, mlstm_f_bias=mlstm_f_bias,
                  mlstm_norm_g=mlstm_norm_g, proj_a=proj_a, proj_b=proj_b, proj_c=proj_c, w_out=w_out,
                  mix_norm_post=mix_norm_post, ffn2_norm_pre=ffn2_norm_pre, ffn2_w_in=ffn2_w_in,
                  ffn2_w_down=ffn2_w_down, ffn2_norm_post=ffn2_norm_post)
    lw = _prep_weights(params)
    nbp, seq, _ = x_prompt.shape
    nbs, steps, _ = x_sample.shape

    yp = x_prompt
    ys = _to_blocked(x_sample)
    hist_all = jnp.pad(state_pool, ((0, 0), (0, 0), (1, 0), (0, 0)))
    n_all = state_mlstm_n.reshape(DEPTH, nbs, HEADS * DQK)
    m_all = jnp.repeat(state_mlstm_m, DQK, axis=2)

    pool_p, c_p, n_p, m_p = [], [], [], []
    pool_s, n_s, m_s, v_s = [], [], [], []
    c_sample = jnp.zeros(state_mlstm_C.shape, F32)
    for l in range(DEPTH):
        ffn1 = (lw['ffn1_norm_pre'], lw['ffn1_w_in'], lw['ffn1_w_down'], lw['ffn1_norm_post'])
        ffn2 = (lw['ffn2_norm_pre'], lw['ffn2_w_in'], lw['ffn2_w_down'], lw['ffn2_norm_post'])
        yp = _ffn(yp.reshape(nbp * seq, D_MODEL), *ffn1, l).reshape(nbp, seq, D_MODEL)
        yp, pb, cf, nf_, mf = _mix_prompt(yp, lw, l)
        yp = _ffn(yp.reshape(nbp * seq, D_MODEL), *ffn2, l).reshape(nbp, seq, D_MODEL)
        pool_p.append(pb[:, 1:, :])
        c_p.append(cf)
        n_p.append(nf_[:, :HEADS, :])
        m_p.append(mf[:, :HEADS, 0])
        ys = _ffn(ys, *ffn1, l)
        hist = _to_blocked(hist_all[l]).reshape(nbs // SAMPLE_SEQS_PER_BLOCK,
                                                POOL_HIST * SAMPLE_SEQS_PER_BLOCK, 512)
        ys, vn, pool_new, c_sample, n_new, m_new = _mix_sample(
            ys, hist, state_mlstm_C, c_sample, n_all[l], m_all[l], lw, l)
        ys = _ffn(ys, *ffn2, l)
        pool_s.append(_from_blocked(pool_new.reshape(nbs * POOL_HIST, 512), POOL_HIST)[:, 1:, :])
        n_s.append(n_new.reshape(nbs, HEADS, DQK))
        m_s.append(m_new.reshape(nbs, HEADS, DQK)[:, :, 0])
        v_s.append(_from_blocked(vn, steps))
    return (yp, _from_blocked(ys, steps),
            jnp.stack(pool_p), jnp.stack(c_p), jnp.stack(n_p), jnp.stack(m_p),
            jnp.stack(pool_s), c_sample, jnp.stack(n_s), jnp.stack(m_s), jnp.stack(v_s))
```

```python
import functools

import jax
import jax.numpy as jnp
from jax import lax
from jax.experimental import pallas as pl
from jax.experimental.pallas import tpu as pltpu

F32 = jnp.float32
BF16 = jnp.bfloat16

D_MODEL = 1024
D_FF = 2816
DEPTH = 4
EPS = 1e-6
PAST_LEN = 16384
N_GROUPS = 4
GROUP_W = 128
POOL_WINDOWS = (2, 4, 8, 16)
POOL_HIST = 16
HEADS = 4
DQK = 128
DV = 256
BLOCK_ROWS = 128
SAMPLE_SEQS_PER_BLOCK = 16
SAMPLE_STEPS = 8

OFF_AU, OFF_AV, OFF_B, OFF_Q, OFF_K, OFF_V, OFF_O = 0, 512, 1024, 1536, 2048, 2560, 3584
WCAT_COLS = 4608
GATE_COLS = 8

SAMPLE_SEQ_UNROLL = 4

FFN_TM = 2048
FFN_TF = 256
MIX_ROW_SPLIT = 2
FFN_ROW_BLOCKS = 4
PROMPT_SEQS_PER_STEP = 4
VMEM_LIMIT_BYTES = 56 * 1024 * 1024


def _dot(a, b):
    return jnp.dot(a, b, preferred_element_type=F32)


def _dot_nt(a, b):
    return lax.dot_general(a, b, (((1,), (1,)), ((), ())), preferred_element_type=F32)


def _dot_tn(a, b):
    return lax.dot_general(a, b, (((0,), (0,)), ((), ())), preferred_element_type=F32)


def _rms(x, g):
    return x * lax.rsqrt(jnp.mean(x * x, axis=-1, keepdims=True) + EPS) * g


def _sigmoid(x):
    return 1.0 / (1.0 + jnp.exp(-x))


def _gelu(x):
    return 0.5 * x * (1.0 + jnp.tanh(0.7978845608028654 * (x + 0.044715 * (x * x * x))))


def _log_sigmoid(x):
    return jnp.minimum(x, 0.0) - jnp.log1p(jnp.exp(-jnp.abs(x)))


def _const_spec(shape, index, ngrid):
    zeros = (0,) * (len(shape) - 1)
    if ngrid == 1:
        imap = lambda i: (index,) + zeros
    else:
        imap = lambda i, j: (index,) + zeros
    return pl.BlockSpec((None,) + tuple(shape[1:]), imap, pipeline_mode=pl.Buffered(1))


def _ffn_tiles(m):
    if m > FFN_TM:
        return FFN_TM, FFN_TF, FFN_ROW_BLOCKS
    return m, D_FF // 2, FFN_ROW_BLOCKS // 2


def _ffn_kernel(x_ref, gpre_ref, wgate_ref, wup_ref, wdown_ref, gpost_ref, o_ref, h_scr, *, row_blocks, nf):
    f = pl.program_id(1)
    last = nf - 1
    tm = x_ref.shape[0]
    rb = tm // row_blocks

    def step(first, final):
        wgate = wgate_ref[...]
        wup = wup_ref[...]
        wdown = wdown_ref[...]
        for r in range(row_blocks):
            rows = slice(r * rb, (r + 1) * rb)
            if first:
                h = _rms(x_ref[rows, :], gpre_ref[...]).astype(BF16)
                h_scr[rows, :] = h
            else:
                h = h_scr[rows, :]
            gate = _dot(h, wgate)
            up = _dot(h, wup)
            act = (gate * _sigmoid(gate) * up).astype(BF16)
            part = _dot(act, wdown)
            acc = part if first else o_ref[rows, :] + part
            if final:
                acc = x_ref[rows, :] + _rms(acc, 0.5 * gpost_ref[...])
            o_ref[rows, :] = acc

    @pl.when(f == 0)
    def _():
        step(True, False)

    if nf > 2:
        @pl.when(jnp.logical_and(f > 0, f < last))
        def _():
            step(False, False)

    @pl.when(f == last)
    def _():
        step(False, True)


def _ffn(x2d, gpre, w_in, wdown, gpost, layer):
    m = x2d.shape[0]
    tm, tf, row_blocks = _ffn_tiles(m)
    nf = D_FF // tf
    return pl.pallas_call(
        functools.partial(_ffn_kernel, row_blocks=row_blocks, nf=nf),
        grid=(m // tm, nf),
        in_specs=[
            pl.BlockSpec((tm, D_MODEL), lambda i, f: (i, 0)),
            pl.BlockSpec((None, 1, D_MODEL), lambda i, f: (layer, 0, 0)),
            pl.BlockSpec((None, D_MODEL, tf), lambda i, f: (layer, 0, f)),
            pl.BlockSpec((None, D_MODEL, tf), lambda i, f: (layer, 0, f + nf)),
            pl.BlockSpec((None, tf, D_MODEL), lambda i, f: (layer, f, 0)),
            pl.BlockSpec((None, 1, D_MODEL), lambda i, f: (layer, 0, 0)),
        ],
        out_specs=pl.BlockSpec((tm, D_MODEL), lambda i, f: (i, 0)),
        out_shape=jax.ShapeDtypeStruct((m, D_MODEL), F32),
        scratch_shapes=[pltpu.VMEM((tm, D_MODEL), BF16)],
        compiler_params=pltpu.CompilerParams(
            dimension_semantics=("parallel", "arbitrary"), vmem_limit_bytes=VMEM_LIMIT_BYTES),
        name="ffn",
    )(x2d, gpre, w_in, w_in, wdown, gpost)


def _prenorm(x, g):
    step = x.shape[0] // MIX_ROW_SPLIT
    pieces = [_rms(x[i * step:(i + 1) * step, :], g).astype(BF16) for i in range(MIX_ROW_SPLIT)]
    return jnp.concatenate(pieces, axis=0), pieces


def _rows_dot(pieces, w):
    return jnp.concatenate([_dot(p, w) for p in pieces], axis=0)


def _branch_a(h, h_pieces, wcat_ref, lng_ref, lnb_ref, mixm_ref, mixb_ref, outa_scr, vn_out_ref=None):
    rows = h.shape[0]
    a_v = _gelu(_rows_dot(h_pieces, wcat_ref[:, OFF_AV:OFF_AV + 512]))
    mu = jnp.mean(a_v, axis=-1, keepdims=True)
    xc = a_v - mu
    var = jnp.mean(xc * xc, axis=-1, keepdims=True)
    v_n = xc * lax.rsqrt(var + EPS) * lng_ref[...] + lnb_ref[...]
    if vn_out_ref is not None:
        vn_out_ref[...] = v_n
    vb = v_n.astype(BF16)
    a_u = _gelu(_dot(h, wcat_ref[:, OFF_AU:OFF_AU + 512]))
    for g in range(N_GROUPS):
        cols = slice(g * GROUP_W, (g + 1) * GROUP_W)
        mix_g = mixm_ref[g]
        bias_g = mixb_ref[g]
        for s in range(rows // BLOCK_ROWS):
            rws = slice(s * BLOCK_ROWS, (s + 1) * BLOCK_ROWS)
            mixed = _dot(mix_g, vb[rws, cols]) + bias_g
            outa_scr[rws, cols] = (a_u[rws, cols] * mixed).astype(BF16)


def _branch_b(b_in, poolw_ref, pscale_ref, hist_in, hist_out, pooled_scr, outb_scr, *, rs, pos1):
    rows = b_in.shape[0]
    hist_rows = POOL_HIST * rs
    for s in range(rows // BLOCK_ROWS):
        rws = slice(s * BLOCK_ROWS, (s + 1) * BLOCK_ROWS)
        cur = b_in[rws, :]
        ext = jnp.concatenate([hist_in(s), cur], axis=0)
        for g, w in enumerate(POOL_WINDOWS):
            cols = slice(g * GROUP_W, (g + 1) * GROUP_W)
            acc = ext[:, cols]
            sh = 1
            while sh < w:
                acc = acc + pltpu.roll(acc, sh * rs, axis=0)
                sh *= 2
            win = acc[hist_rows:, :]
            if pos1 is None:
                mean = win * (1.0 / w)
            else:
                mean = win / jnp.minimum(pos1, w).astype(F32)
            pooled_scr[rws, cols] = (mean - cur[:, cols]).astype(BF16)
        hist_out(s, ext[BLOCK_ROWS:, :])
    for g in range(N_GROUPS):
        cols = slice(g * GROUP_W, (g + 1) * GROUP_W)
        mixed = _dot(pooled_scr[:, cols], poolw_ref[g]) * pscale_ref[:, cols]
        outb_scr[:, cols] = mixed.astype(BF16)


def _gate_weight(j, width, wcat_ref, wg_ref):
    lo = j * width
    if lo < 3 * D_MODEL:
        return wg_ref[:, lo:lo + width]
    lo -= 3 * D_MODEL
    return wcat_ref[:, OFF_O + lo:OFF_O + lo + width]


def _merge(x, gate, hc, pa_ref, pb_ref, pc_ref, wout_ref, gpost_ref, outa_scr, outb_scr):
    merged = gate(0) * _dot(outa_scr[...], pa_ref[...])
    merged += gate(1) * _dot(outb_scr[...], pb_ref[...])
    out_c = (gate(3) * hc).astype(BF16)
    merged += gate(2) * _dot(out_c, pc_ref[...])
    mb = merged.astype(BF16)
    step = x.shape[0] // MIX_ROW_SPLIT
    outs = []
    for i in range(MIX_ROW_SPLIT):
        rws = slice(i * step, (i + 1) * step)
        outs.append(x[rws, :] + _rms(_dot(mb[rws, :], wout_ref[...]), gpost_ref[...]))
    return jnp.concatenate(outs, axis=0)


def _scan_lanes(x, op, fill):
    lane = lax.broadcasted_iota(jnp.int32, x.shape, 1)
    sh = 1
    while sh < x.shape[1]:
        x = op(x, jnp.where(lane >= sh, pltpu.roll(x, sh, axis=1), fill))
        sh *= 2
    return x


def _last_lane(x):
    lane = lax.broadcasted_iota(jnp.int32, x.shape, 1)
    return jnp.max(jnp.where(lane == x.shape[1] - 1, x, -jnp.inf), axis=1, keepdims=True)


def _mlstm_gates(h_pieces, wift_ref, gbias_ref, m_ref, *, nseq):
    t_len = BLOCK_ROWS
    grow = jnp.concatenate([_dot_nt(wift_ref[...], p) for p in h_pieces], axis=1)
    seqs = range(nseq)
    ig = jnp.concatenate([grow[0:16, s * t_len:(s + 1) * t_len] + gbias_ref[0:16, :] for s in seqs], axis=0)
    lf = _log_sigmoid(
        jnp.concatenate([grow[16:32, s * t_len:(s + 1) * t_len] + gbias_ref[16:32, :] for s in seqs], axis=0))
    m0 = m_ref[...].reshape(nseq * 16, t_len)
    b = _scan_lanes(lf, jnp.add, 0.0)
    a = ig - b
    m_t = b + jnp.maximum(m0, _scan_lanes(a, jnp.maximum, -jnp.inf))
    inter = jnp.exp(b + m0 - m_t)
    b_last = _last_lane(b)
    m_end = _last_lane(m_t)
    decay = jnp.exp(b_last + m0 - m_end)
    w_in = jnp.exp(a + b_last - m_end)
    m_ref[...] = jnp.broadcast_to(m_end, (nseq * 16, t_len)).reshape(nseq, 16, t_len)
    stacked = jnp.concatenate([b - m_t, inter, jnp.exp(-m_t), w_in], axis=0)
    pad = (-stacked.shape[0]) % t_len
    if pad:
        stacked = jnp.concatenate([stacked, jnp.zeros((pad, t_len), F32)], axis=0)
    cols_t = [stacked[i * t_len:(i + 1) * t_len, :].T for i in range(stacked.shape[0] // t_len)]

    def col(quantity, s, hd):
        idx = quantity * nseq * 16 + s * 16 + hd
        return cols_t[idx // t_len][:, idx % t_len:idx % t_len + 1]

    return a, decay, col


def _qkv(h, wcat_ref):
    q = _dot(h, wcat_ref[:, OFF_Q:OFF_Q + 512])
    k = _dot(h, wcat_ref[:, OFF_K:OFF_K + 512]) * (DQK ** -0.5)
    v = _dot(h, wcat_ref[:, OFF_V:OFF_V + D_MODEL])
    return q, k, v


def _mlstm_prompt(qkv, gates, normg_ref, c_ref, n_ref, hc_scr, *, nseq, fillers):
    t_len = BLOCK_ROWS
    a_all, decay_all, col = gates
    q, k, v = qkv
    ri = lax.broadcasted_iota(jnp.int32, (t_len, t_len), 0)
    ci = lax.broadcasted_iota(jnp.int32, (t_len, t_len), 1)
    causal = ci <= ri
    for s in range(nseq):
        rws = slice(s * t_len, (s + 1) * t_len)
        a = a_all[s * 16:(s + 1) * 16, :]
        decay = decay_all[s * 16:(s + 1) * 16, :]
        for hd in range(HEADS):
            fillers[s * HEADS + hd]()
            qh = q[rws, hd * DQK:(hd + 1) * DQK]
            kh = k[rws, hd * DQK:(hd + 1) * DQK]
            vh = v[rws, hd * DV:(hd + 1) * DV]
            qb = qh.astype(BF16)
            kb = kh.astype(BF16)
            bm_c = col(0, s, hd)
            inter_c = col(1, s, hd)
            en_c = col(2, s, hd)
            win_c = col(3, s, hd)
            dmat = jnp.exp(jnp.where(causal, bm_c + a[hd:hd + 1, :], -jnp.inf))
            wmat = _dot_nt(qb, kb) * dmat
            c_old = c_ref[s, hd]
            n_old = n_ref[s, hd:hd + 1, :]
            num = _dot(wmat.astype(BF16), vh.astype(BF16)) + inter_c * _dot_nt(qb, c_old.astype(BF16))
            den = jnp.sum(wmat, axis=1, keepdims=True) + inter_c * jnp.sum(qh * n_old, axis=1, keepdims=True)
            hout = num / jnp.maximum(jnp.abs(den), en_c)
            hc_scr[rws, hd * DV:(hd + 1) * DV] = _rms(hout, normg_ref[:, hd * DV:(hd + 1) * DV])
            dec_row = decay[hd:hd + 1, :]
            c_ref[s, hd] = dec_row * c_old + _dot_tn((vh * win_c).astype(BF16), kb)
            n_ref[s, hd:hd + 1, :] = dec_row * n_old + jnp.sum(kh * win_c, axis=0, keepdims=True)


def _mix_prompt_kernel(x_ref, xnext_ref, gpre_ref, wcat_ref, wg_ref, wift_ref, gbias_ref, lng_ref, lnb_ref,
                       mixm_ref, mixb_ref, poolw_ref, pscale_ref, normg_ref, pa_ref, pb_ref, pc_ref, wout_ref,
                       gpost_ref,
                       o_ref, pool_ref, c_ref, n_ref, m_ref,
                       outa_scr, outb_scr, pooled_scr, hc_scr, gate_scr, h_scr, *, nseq):
    c = pl.program_id(1)
    rows = nseq * BLOCK_ROWS

    @pl.when(c == 0)
    def _():
        pool_ref[...] = jnp.zeros(pool_ref.shape, F32)
        c_ref[...] = jnp.zeros(c_ref.shape, F32)
        n_ref[...] = jnp.zeros(n_ref.shape, F32)
        m_ref[...] = jnp.zeros(m_ref.shape, F32)

    slot = c % 2

    @pl.when(c == 0)
    def _():
        h_scr[0] = _rms(x_ref[...].reshape(rows, D_MODEL), gpre_ref[...]).astype(BF16)

    x = x_ref[...].reshape(rows, D_MODEL)
    h = h_scr[slot]

    gates = _mlstm_gates([h], wift_ref, gbias_ref, m_ref, nseq=nseq)

    _branch_a(h, [h], wcat_ref, lng_ref, lnb_ref, mixm_ref, mixb_ref, outa_scr)

    b_in = _dot(h, wcat_ref[:, OFF_B:OFF_B + 512])
    qkv = _qkv(h, wcat_ref)
    h_scr[1 - slot] = _rms(xnext_ref[...].reshape(rows, D_MODEL), gpre_ref[...]).astype(BF16)

    pos1 = c * BLOCK_ROWS + 1 + lax.broadcasted_iota(jnp.int32, (BLOCK_ROWS, GROUP_W), 0)

    def hist_out(s, val):
        pool_ref[s] = val

    _branch_b(b_in, poolw_ref, pscale_ref, lambda s: pool_ref[s], hist_out, pooled_scr, outb_scr,
              rs=1, pos1=pos1)

    n_fill = nseq * HEADS
    fill_w = 4 * D_MODEL // n_fill

    def make_filler(j):
        def fill():
            gate_scr[:, j * fill_w:(j + 1) * fill_w] = _sigmoid(_dot(h, _gate_weight(j, fill_w, wcat_ref, wg_ref)))
        return fill

    _mlstm_prompt(qkv, gates, normg_ref, c_ref, n_ref, hc_scr, nseq=nseq,
                  fillers=[make_filler(j) for j in range(n_fill)])

    y = _merge(x, lambda j: gate_scr[:, j * D_MODEL:(j + 1) * D_MODEL], hc_scr[...],
               pa_ref, pb_ref, pc_ref, wout_ref, gpost_ref, outa_scr, outb_scr)
    o_ref[...] = y.reshape(nseq, BLOCK_ROWS, D_MODEL)


def _mix_prompt(x3d, lw, layer):
    nb, seq, _ = x3d.shape
    nseq = PROMPT_SEQS_PER_STEP
    rows = nseq * BLOCK_ROWS
    nchunks = seq // BLOCK_ROWS
    cs = functools.partial(_const_spec, index=layer, ngrid=2)
    weights = [lw['mix_norm_pre'], lw['wcat'], lw['wg'], lw['wift'], lw['gbias'], lw['ln_g'], lw['ln_b'],
               lw['mixm_p'], lw['mixb_p'], lw['pool_w'], lw['pool_scale'], lw['norm_g'],
               lw['proj_a'], lw['proj_b'], lw['proj_c'], lw['w_out'], lw['mix_norm_post']]
    out_shapes = (
        jax.ShapeDtypeStruct((nb, seq, D_MODEL), F32),
        jax.ShapeDtypeStruct((nb, POOL_HIST, 512), F32),
        jax.ShapeDtypeStruct((nb, HEADS, DV, DQK), F32),
        jax.ShapeDtypeStruct((nb, 8, DQK), F32),
        jax.ShapeDtypeStruct((nb, 16, BLOCK_ROWS), F32),
    )
    out_specs = (
        pl.BlockSpec((nseq, BLOCK_ROWS, D_MODEL), lambda g, c: (g, c, 0)),
        pl.BlockSpec((nseq, POOL_HIST, 512), lambda g, c: (g, 0, 0)),
        pl.BlockSpec((nseq, HEADS, DV, DQK), lambda g, c: (g, 0, 0, 0)),
        pl.BlockSpec((nseq, 8, DQK), lambda g, c: (g, 0, 0)),
        pl.BlockSpec((nseq, 16, BLOCK_ROWS), lambda g, c: (g, 0, 0)),
    )
    return pl.pallas_call(
        functools.partial(_mix_prompt_kernel, nseq=nseq),
        grid=(nb // nseq, nchunks),
        in_specs=[pl.BlockSpec((nseq, BLOCK_ROWS, D_MODEL), lambda g, c: (g, c, 0)),
                  pl.BlockSpec((nseq, BLOCK_ROWS, D_MODEL), lambda g, c: (g, jnp.minimum(c + 1, nchunks - 1), 0))]
        + [cs(w.shape) for w in weights],
        out_specs=out_specs,
        out_shape=out_shapes,
        scratch_shapes=[pltpu.VMEM((rows, 512), BF16), pltpu.VMEM((rows, 512), BF16),
                        pltpu.VMEM((rows, 512), BF16), pltpu.VMEM((rows, D_MODEL), F32),
                        pltpu.VMEM((rows, 4 * D_MODEL), F32), pltpu.VMEM((2, rows, D_MODEL), BF16)],
        compiler_params=pltpu.CompilerParams(
            dimension_semantics=("parallel", "arbitrary"), vmem_limit_bytes=VMEM_LIMIT_BYTES),
        name="mix_prompt",
    )(x3d, x3d, *weights)


def _mix_s1_kernel(x_ref, hist_ref, gpre_ref, wcat_ref, wifc_ref, lng_ref, lnb_ref, mixm_ref, mixb_ref,
                   poolw_ref, pscale_ref,
                   outa_ref, outb_ref, q_ref, k_ref, v_ref, gi_ref, gf_ref, vn_ref, pool_ref,
                   pooled_scr):
    h, h_pieces = _prenorm(x_ref[...], gpre_ref[...])
    _branch_a(h, h_pieces, wcat_ref, lng_ref, lnb_ref, mixm_ref, mixb_ref, outa_ref, vn_out_ref=vn_ref)

    def hist_out(s, val):
        pool_ref[s] = val

    b_in = _dot(h, wcat_ref[:, OFF_B:OFF_B + 512])
    q, k, v = _qkv(h, wcat_ref)
    _branch_b(b_in, poolw_ref, pscale_ref, lambda s: hist_ref[s], hist_out, pooled_scr, outb_ref,
              rs=SAMPLE_SEQS_PER_BLOCK, pos1=None)
    for j in range(512 // 128):
        q_ref[j] = q[:, j * 128:(j + 1) * 128]
        k_ref[j] = k[:, j * 128:(j + 1) * 128]
    for j in range(D_MODEL // 128):
        v_ref[j] = v[:, j * 128:(j + 1) * 128]
    gi_ref[...] = _dot(h, wifc_ref[:, 0:HEADS * DQK])
    gf_ref[...] = _dot(h, wifc_ref[:, HEADS * DQK:2 * HEADS * DQK])


def _mix_s2_kernel(q_ref, k_ref, v_ref, gi_ref, gf_ref, ib_ref, fb_ref, normg_ref, c_ref, n_ref, m_ref,
                   cbuf_ref, hc_ref, cn_ref, nn_ref, mn_ref,
                   cq_scr, win_scr, dec_scr):
    del cbuf_ref
    nb = SAMPLE_SEQS_PER_BLOCK
    steps = SAMPLE_STEPS

    vblocks = DV // 128

    def slab(ref, t):
        return ref[t * nb:(t + 1) * nb, :]

    def slab_v(ref, hd, t):
        return jnp.concatenate([ref[hd * vblocks + j, t * nb:(t + 1) * nb, :] for j in range(vblocks)], axis=1)

    def twice(x):
        return jnp.concatenate([x] * vblocks, axis=1)

    m0 = m_ref[...]
    ig = [slab(gi_ref, t) + ib_ref[...] for t in range(steps)]
    lf = [_log_sigmoid(slab(gf_ref, t) + fb_ref[...]) for t in range(steps)]
    b, a, m_t = [], [], []
    run_b = jnp.zeros_like(m0)
    run_max = jnp.full(m0.shape, -jnp.inf, F32)
    for t in range(steps):
        run_b = run_b + lf[t]
        b.append(run_b)
        a.append(ig[t] - run_b)
        run_max = jnp.maximum(run_max, a[t])
        m_t.append(run_b + jnp.maximum(m0, run_max))
    inter = [jnp.exp(b[t] + m0 - m_t[t]) for t in range(steps)]
    m_end = m_t[-1]
    decay = jnp.exp(b[-1] + m0 - m_end)
    w_in = [jnp.exp(a[t] + b[-1] - m_end) for t in range(steps)]
    mn_ref[...] = m_end
    for hd in range(HEADS):
        hcols = slice(hd * DQK, (hd + 1) * DQK)
        for t in range(steps):
            win_scr[hd, t * nb:(t + 1) * nb, :] = w_in[t][:, hcols]
            dec_scr[hd, t * nb:(t + 1) * nb, :] = decay[:, hcols]

    def seq_body(bi, carry):
        rows = pl.ds(bi, steps, stride=nb)
        for hd in range(HEADS):
            c_old = c_ref[bi, hd]
            qb = q_ref[hd, rows, :].astype(BF16)
            cq = _dot_nt(qb, c_old.astype(BF16))
            for j in range(vblocks):
                cq_scr[hd * vblocks + j, rows, :] = cq[:, j * 128:(j + 1) * 128]
            vrows = jnp.concatenate([v_ref[hd * vblocks + j, rows, :] for j in range(vblocks)], axis=1)
            vs = (vrows * twice(win_scr[hd, rows, :])).astype(BF16)
            kb = k_ref[hd, rows, :].astype(BF16)
            dec_row = dec_scr[hd, rows, :][0:1, :]
            cn_ref[bi, hd] = dec_row * c_old + _dot_tn(vs, kb)
        return carry

    lax.fori_loop(0, nb, seq_body, 0, unroll=SAMPLE_SEQ_UNROLL)

    pairs = [(t, s) for t in range(steps) for s in range(t + 1)]
    dfac = [jnp.exp((b[t] - m_t[t]) + a[s]) for (t, s) in pairs]
    enm = [jnp.exp(-m_t[t]) for t in range(steps)]
    for hd in range(HEADS):
        hcols = slice(hd * DQK, (hd + 1) * DQK)
        vcols = slice(hd * DV, (hd + 1) * DV)
        qs = [slab(q_ref.at[hd], t) for t in range(steps)]
        ks = [slab(k_ref.at[hd], t) for t in range(steps)]
        vs = [slab_v(v_ref, hd, t) for t in range(steps)]
        n_old = n_ref[:, hcols]
        n_new = decay[:, hcols] * n_old
        for t in range(steps):
            n_new = n_new + w_in[t][:, hcols] * ks[t]
        nn_ref[:, hcols] = n_new
        prods = jnp.concatenate([qs[t] * ks[s] for (t, s) in pairs] + [qs[t] * n_old for t in range(steps)], axis=0)
        red = jnp.sum(prods, axis=1, keepdims=True)
        wts = red[:len(pairs) * nb] * jnp.concatenate([d[:, hcols] for d in dfac], axis=0)
        for t in range(steps):
            inter_t = inter[t][:, hcols]
            num = twice(inter_t) * slab_v(cq_scr, hd, t)
            den = inter_t * red[(len(pairs) + t) * nb:(len(pairs) + t + 1) * nb]
            for s in range(t + 1):
                p = pairs.index((t, s))
                w_ts = wts[p * nb:(p + 1) * nb]
                num = num + twice(w_ts) * vs[s]
                den = den + w_ts
            hout = num / twice(jnp.maximum(jnp.abs(den), enm[t][:, hcols]))
            hc_ref[t * nb:(t + 1) * nb, vcols] = _rms(hout, normg_ref[:, vcols])


def _mix_s3_kernel(x_ref, outa_ref, outb_ref, hc_ref, gpre_ref, wcat_ref, wg_ref, pa_ref, pb_ref, pc_ref,
                   wout_ref, gpost_ref, o_ref):
    x = x_ref[...]
    h, h_pieces = _prenorm(x, gpre_ref[...])

    def gate(j):
        pieces = h_pieces if j == 0 else [h]
        return _sigmoid(_rows_dot(pieces, _gate_weight(j, D_MODEL, wcat_ref, wg_ref)))

    o_ref[...] = _merge(x, gate, hc_ref[...], pa_ref, pb_ref, pc_ref, wout_ref, gpost_ref, outa_ref, outb_ref)


def _mix_sample(xs, hist, c_all, c_buf, n_in, m_in, lw, layer):
    rows = xs.shape[0]
    nblk = rows // BLOCK_ROWS
    hist_rows = POOL_HIST * SAMPLE_SEQS_PER_BLOCK
    cs1 = functools.partial(_const_spec, index=layer, ngrid=1)
    full = lambda shape: pl.BlockSpec(shape, lambda i: (0,) * len(shape))
    params = pltpu.CompilerParams(dimension_semantics=("arbitrary",), vmem_limit_bytes=VMEM_LIMIT_BYTES)

    w1 = [lw['mix_norm_pre'], lw['wcat'], lw['wifc'], lw['ln_g'], lw['ln_b'], lw['mixm_s'], lw['mixb_s'],
          lw['pool_w'], lw['pool_scale']]
    s1_shapes = (
        jax.ShapeDtypeStruct((rows, 512), BF16),
        jax.ShapeDtypeStruct((rows, 512), BF16),
        jax.ShapeDtypeStruct((HEADS, rows, DQK), F32),
        jax.ShapeDtypeStruct((HEADS, rows, DQK), F32),
        jax.ShapeDtypeStruct((D_MODEL // 128, rows, 128), F32),
        jax.ShapeDtypeStruct((rows, HEADS * DQK), F32),
        jax.ShapeDtypeStruct((rows, HEADS * DQK), F32),
        jax.ShapeDtypeStruct((rows, 512), F32),
        jax.ShapeDtypeStruct((nblk, hist_rows, 512), F32),
    )
    outa, outb, q, k, v, gi, gf, vn, pool_new = pl.pallas_call(
        _mix_s1_kernel,
        grid=(1,),
        in_specs=[full(xs.shape), full(hist.shape)] + [cs1(w.shape) for w in w1],
        out_specs=tuple(full(s.shape) for s in s1_shapes),
        out_shape=s1_shapes,
        scratch_shapes=[pltpu.VMEM((rows, 512), BF16)],
        compiler_params=params,
        name="mix_s1",
    )(xs, hist, *w1)

    nb = SAMPLE_SEQS_PER_BLOCK
    blk = lambda width: pl.BlockSpec((BLOCK_ROWS, width), lambda j: (j, 0))
    seqblk = lambda width: pl.BlockSpec((nb, width), lambda j: (j, 0))
    cblk = pl.BlockSpec((None, nb, HEADS, DV, DQK), lambda j: (layer, j, 0, 0, 0))
    lrow = lambda width: pl.BlockSpec((None, 1, width), lambda j: (layer, 0, 0))
    colblk = lambda n: pl.BlockSpec((n, BLOCK_ROWS, 128), lambda j: (0, j, 0))
    hc, c_buf, n_new, m_new = pl.pallas_call(
        _mix_s2_kernel,
        grid=(nblk,),
        in_specs=[colblk(HEADS), colblk(HEADS), colblk(D_MODEL // 128), blk(512), blk(512),
                  lrow(512), lrow(512), lrow(D_MODEL), cblk, seqblk(512), seqblk(512),
                  pl.BlockSpec(memory_space=pl.ANY)],
        out_specs=(blk(D_MODEL), cblk, seqblk(512), seqblk(512)),
        out_shape=(jax.ShapeDtypeStruct((rows, D_MODEL), F32),
                   jax.ShapeDtypeStruct(c_all.shape, F32),
                   jax.ShapeDtypeStruct(n_in.shape, F32),
                   jax.ShapeDtypeStruct(m_in.shape, F32)),
        input_output_aliases={11: 1},
        scratch_shapes=[pltpu.VMEM((D_MODEL // 128, BLOCK_ROWS, 128), F32),
                        pltpu.VMEM((HEADS, BLOCK_ROWS, 128), F32), pltpu.VMEM((HEADS, BLOCK_ROWS, 128), F32)],
        compiler_params=pltpu.CompilerParams(dimension_semantics=("parallel",),
                                             vmem_limit_bytes=VMEM_LIMIT_BYTES),
        name="mix_s2",
    )(q, k, v, gi, gf, lw['ib'], lw['fb'], lw['norm_g'], c_all, n_in, m_in, c_buf)

    w3 = [lw['mix_norm_pre'], lw['wcat'], lw['wg'], lw['proj_a'], lw['proj_b'], lw['proj_c'], lw['w_out'],
          lw['mix_norm_post']]
    y = pl.pallas_call(
        _mix_s3_kernel,
        grid=(1,),
        in_specs=[full(xs.shape), full(outa.shape), full(outb.shape), full(hc.shape)]
        + [cs1(w.shape) for w in w3],
        out_specs=full(xs.shape),
        out_shape=jax.ShapeDtypeStruct(xs.shape, F32),
        compiler_params=params,
        name="mix_s3",
    )(xs, outa, outb, hc, *w3)
    return y, vn, pool_new, c_buf, n_new, m_new


def _to_blocked(x):
    nseq, t, c = x.shape
    nb = SAMPLE_SEQS_PER_BLOCK
    return x.reshape(nseq // nb, nb, t, c).transpose(0, 2, 1, 3).reshape(nseq * t, c)


def _from_blocked(x, t):
    rows, c = x.shape
    nb = SAMPLE_SEQS_PER_BLOCK
    nseq = rows // t
    return x.reshape(nseq // nb, t, nb, c).transpose(0, 2, 1, 3).reshape(nseq, t, c)


def _prep_weights(p):
    w_in = p['w_in']
    row = lambda a: a.reshape(DEPTH, 1, -1).astype(F32)
    wi = w_in[:, :, 4608:4612]
    wf = w_in[:, :, 4612:4616]
    wift = jnp.zeros((DEPTH, 32, D_MODEL), F32)
    wift = wift.at[:, 0:HEADS].set(wi.transpose(0, 2, 1)).at[:, 16:16 + HEADS].set(wf.transpose(0, 2, 1))
    wifc = jnp.concatenate([jnp.repeat(wi, DQK, axis=2), jnp.repeat(wf, DQK, axis=2)], axis=2)
    gbias = jnp.zeros((DEPTH, 32, BLOCK_ROWS), F32)
    gbias = gbias.at[:, 0:HEADS].set(jnp.broadcast_to(p['mlstm_i_bias'][:, :, None], (DEPTH, HEADS, BLOCK_ROWS)))
    gbias = gbias.at[:, 16:16 + HEADS].set(
        jnp.broadcast_to(p['mlstm_f_bias'][:, :, None], (DEPTH, HEADS, BLOCK_ROWS)))
    lane_pad = lambda a: jnp.repeat(a.astype(F32), DQK, axis=1).reshape(DEPTH, 1, HEADS * DQK)

    ws = p['gmlp_w_s']
    bs = p['gmlp_b_s']
    mixm_p = jnp.tril(ws)
    mixb_p = jnp.broadcast_to(bs[..., None], ws.shape)
    nb = SAMPLE_SEQS_PER_BLOCK
    tril_s = jnp.tril(ws[:, :, :SAMPLE_STEPS, :SAMPLE_STEPS])
    eye = jnp.eye(nb, dtype=F32)
    mixm_s = jnp.einsum('lgts,bc->lgtbsc', tril_s, eye).reshape(DEPTH, N_GROUPS, BLOCK_ROWS, BLOCK_ROWS)
    mixb_s = jnp.broadcast_to(jnp.repeat(bs[:, :, :SAMPLE_STEPS], nb, axis=2)[..., None], ws.shape)

    return dict(
        ffn1_norm_pre=row(p['ffn1_norm_pre']), ffn1_w_in=p['ffn1_w_in'].astype(BF16),
        ffn1_w_down=p['ffn1_w_down'].astype(BF16), ffn1_norm_post=row(p['ffn1_norm_post']),
        ffn2_norm_pre=row(p['ffn2_norm_pre']), ffn2_w_in=p['ffn2_w_in'].astype(BF16),
        ffn2_w_down=p['ffn2_w_down'].astype(BF16), ffn2_norm_post=row(p['ffn2_norm_post']),
        mix_norm_pre=row(p['mix_norm_pre']), mix_norm_post=row(p['mix_norm_post']),
        wcat=w_in[:, :, :WCAT_COLS].astype(BF16), wg=w_in[:, :, WCAT_COLS + GATE_COLS:].astype(BF16),
        wift=wift.astype(BF16), wifc=wifc.astype(BF16), gbias=gbias,
        ib=lane_pad(p['mlstm_i_bias']), fb=lane_pad(p['mlstm_f_bias']),
        ln_g=row(p['gmlp_ln_g']), ln_b=row(p['gmlp_ln_b']),
        mixm_p=mixm_p.astype(BF16), mixb_p=mixb_p.astype(F32),
        mixm_s=mixm_s.astype(BF16), mixb_s=mixb_s.astype(F32),
        pool_w=p['pool_w'].astype(BF16), pool_scale=row(p['pool_scale']), norm_g=row(p['mlstm_norm_g']),
        proj_a=p['proj_a'].astype(BF16), proj_b=p['proj_b'].astype(BF16), proj_c=p['proj_c'].astype(BF16),
        w_out=p['w_out'].astype(BF16),
    )


def kernel(x_prompt, x_sample, state_pool, state_mlstm_C, state_mlstm_n, state_mlstm_m, ffn1_norm_pre, ffn1_w_in, ffn1_w_down, ffn1_norm_post, mix_norm_pre, w_in, gmlp_ln_g, gmlp_ln_b, gmlp_w_s, gmlp_b_s, pool_w, pool_scale, mlstm_i_bias, mlstm_f_bias, mlstm_norm_g, proj_a, proj_b, proj_c, w_out, mix_norm_post, ffn2_norm_pre, ffn2_w_in, ffn2_w_down, ffn2_norm_post):
    params = dict(ffn1_norm_pre=ffn1_norm_pre, ffn1_w_in=ffn1_w_in, ffn1_w_down=ffn1_w_down,
                  ffn1_norm_post=ffn1_norm_post, mix_norm_pre=mix_norm_pre, w_in=w_in, gmlp_ln_g=gmlp_ln_g,
                  gmlp_ln_b=gmlp_ln_b, gmlp_w_s=gmlp_w_s, gmlp_b_s=gmlp_b_s, pool_w=pool_w,
                  pool_scale=pool_scale, mlstm_i_bias=mlstm_i_bias, mlstm_f_bias=mlstm_f_bias,
                  mlstm_norm_g=mlstm_norm_g, proj_a=proj_a, proj_b=proj_b, proj_c=proj_c, w_out=w_out,
                  mix_norm_post=mix_norm_post, ffn2_norm_pre=ffn2_norm_pre, ffn2_w_in=ffn2_w_in,
                  ffn2_w_down=ffn2_w_down, ffn2_norm_post=ffn2_norm_post)
    lw = _prep_weights(params)
    nbp, seq, _ = x_prompt.shape
    nbs, steps, _ = x_sample.shape

    yp = x_prompt
    ys = _to_blocked(x_sample)
    hist_all = jnp.pad(state_pool, ((0, 0), (0, 0), (1, 0), (0, 0)))
    n_all = state_mlstm_n.reshape(DEPTH, nbs, HEADS * DQK)
    m_all = jnp.repeat(state_mlstm_m, DQK, axis=2)

    pool_p, c_p, n_p, m_p = [], [], [], []
    pool_s, n_s, m_s, v_s = [], [], [], []
    c_sample = jnp.zeros(state_mlstm_C.shape, F32)
    for l in range(DEPTH):
        ffn1 = (lw['ffn1_norm_pre'], lw['ffn1_w_in'], lw['ffn1_w_down'], lw['ffn1_norm_post'])
        ffn2 = (lw['ffn2_norm_pre'], lw['ffn2_w_in'], lw['ffn2_w_down'], lw['ffn2_norm_post'])
        yp = _ffn(yp.reshape(nbp * seq, D_MODEL), *ffn1, l).reshape(nbp, seq, D_MODEL)
        yp, pb, cf, nf_, mf = _mix_prompt(yp, lw, l)
        yp = _ffn(yp.reshape(nbp * seq, D_MODEL), *ffn2, l).reshape(nbp, seq, D_MODEL)
        pool_p.append(pb[:, 1:, :])
        c_p.append(cf)
        n_p.append(nf_[:, :HEADS, :])
        m_p.append(mf[:, :HEADS, 0])
        ys = _ffn(ys, *ffn1, l)
        hist = _to_blocked(hist_all[l]).reshape(nbs // SAMPLE_SEQS_PER_BLOCK,
                                                POOL_HIST * SAMPLE_SEQS_PER_BLOCK, 512)
        ys, vn, pool_new, c_sample, n_new, m_new = _mix_sample(
            ys, hist, state_mlstm_C, c_sample, n_all[l], m_all[l], lw, l)
        ys = _ffn(ys, *ffn2, l)
        pool_s.append(_from_blocked(pool_new.reshape(nbs * POOL_HIST, 512), POOL_HIST)[:, 1:, :])
        n_s.append(n_new.reshape(nbs, HEADS, DQK))
        m_s.append(m_new.reshape(nbs, HEADS, DQK)[:, :, 0])
        v_s.append(_from_blocked(vn, steps))
    return (yp, _from_blocked(ys, steps),
            jnp.stack(pool_p), jnp.stack(c_p), jnp.stack(n_p), jnp.stack(m_p),
            jnp.stack(pool_s), c_sample, jnp.stack(n_s), jnp.stack(m_s), jnp.stack(v_s))
```

```python
import functools

import jax
import jax.numpy as jnp
from jax import lax
from jax.experimental import pallas as pl
from jax.experimental.pallas import tpu as pltpu

F32 = jnp.float32
BF16 = jnp.bfloat16

D_MODEL = 1024
D_FF = 2816
DEPTH = 4
EPS = 1e-6
PAST_LEN = 16384
N_GROUPS = 4
GROUP_W = 128
POOL_WINDOWS = (2, 4, 8, 16)
POOL_HIST = 16
HEADS = 4
DQK = 128
DV = 256
BLOCK_ROWS = 128
SAMPLE_SEQS_PER_BLOCK = 16
SAMPLE_STEPS = 8

OFF_AU, OFF_AV, OFF_B, OFF_Q, OFF_K, OFF_V, OFF_O = 0, 512, 1024, 1536, 2048, 2560, 3584
WCAT_COLS = 4608
GATE_COLS = 8

SAMPLE_SEQ_UNROLL = 4

FFN_TM = 2048
FFN_TF = 256
MIX_ROW_SPLIT = 2
FFN_ROW_BLOCKS = 4
PROMPT_SEQS_PER_STEP = 4
VMEM_LIMIT_BYTES = 56 * 1024 * 1024


def _dot(a, b):
    return jnp.dot(a, b, preferred_element_type=F32)


def _dot_nt(a, b):
    return lax.dot_general(a, b, (((1,), (1,)), ((), ())), preferred_element_type=F32)


def _dot_tn(a, b):
    return lax.dot_general(a, b, (((0,), (0,)), ((), ())), preferred_element_type=F32)


def _rms(x, g):
    return x * lax.rsqrt(jnp.mean(x * x, axis=-1, keepdims=True) + EPS) * g


def _sigmoid(x):
    return 1.0 / (1.0 + jnp.exp(-x))


def _gelu(x):
    return 0.5 * x * (1.0 + jnp.tanh(0.7978845608028654 * (x + 0.044715 * (x * x * x))))


def _log_sigmoid(x):
    return jnp.minimum(x, 0.0) - jnp.log1p(jnp.exp(-jnp.abs(x)))


def _const_spec(shape, index, ngrid):
    zeros = (0,) * (len(shape) - 1)
    if ngrid == 1:
        imap = lambda i: (index,) + zeros
    else:
        imap = lambda i, j: (index,) + zeros
    return pl.BlockSpec((None,) + tuple(shape[1:]), imap, pipeline_mode=pl.Buffered(1))


def _ffn_tiles(m):
    if m > FFN_TM:
        return FFN_TM, FFN_TF, FFN_ROW_BLOCKS
    return m, D_FF // 2, FFN_ROW_BLOCKS // 2


def _ffn_kernel(x_ref, gpre_ref, wgate_ref, wup_ref, wdown_ref, gpost_ref, o_ref, h_scr, *, row_blocks, nf):
    f = pl.program_id(1)
    last = nf - 1
    tm = x_ref.shape[0]
    rb = tm // row_blocks

    def step(first, final):
        wgate = wgate_ref[...]
        wup = wup_ref[...]
        wdown = wdown_ref[...]
        for r in range(row_blocks):
            rows = slice(r * rb, (r + 1) * rb)
            if first:
                h = _rms(x_ref[rows, :], gpre_ref[...]).astype(BF16)
                h_scr[rows, :] = h
            else:
                h = h_scr[rows, :]
            gate = _dot(h, wgate)
            up = _dot(h, wup)
            act = (gate * _sigmoid(gate) * up).astype(BF16)
            part = _dot(act, wdown)
            acc = part if first else o_ref[rows, :] + part
            if final:
                acc = x_ref[rows, :] + _rms(acc, 0.5 * gpost_ref[...])
            o_ref[rows, :] = acc

    @pl.when(f == 0)
    def _():
        step(True, False)

    if nf > 2:
        @pl.when(jnp.logical_and(f > 0, f < last))
        def _():
            step(False, False)

    @pl.when(f == last)
    def _():
        step(False, True)


def _ffn(x2d, gpre, w_in, wdown, gpost, layer):
    m = x2d.shape[0]
    tm, tf, row_blocks = _ffn_tiles(m)
    nf = D_FF // tf
    return pl.pallas_call(
        functools.partial(_ffn_kernel, row_blocks=row_blocks, nf=nf),
        grid=(m // tm, nf),
        in_specs=[
            pl.BlockSpec((tm, D_MODEL), lambda i, f: (i, 0)),
            pl.BlockSpec((None, 1, D_MODEL), lambda i, f: (layer, 0, 0)),
            pl.BlockSpec((None, D_MODEL, tf), lambda i, f: (layer, 0, f)),
            pl.BlockSpec((None, D_MODEL, tf), lambda i, f: (layer, 0, f + nf)),
            pl.BlockSpec((None, tf, D_MODEL), lambda i, f: (layer, f, 0)),
            pl.BlockSpec((None, 1, D_MODEL), lambda i, f: (layer, 0, 0)),
        ],
        out_specs=pl.BlockSpec((tm, D_MODEL), lambda i, f: (i, 0)),
        out_shape=jax.ShapeDtypeStruct((m, D_MODEL), F32),
        scratch_shapes=[pltpu.VMEM((tm, D_MODEL), BF16)],
        compiler_params=pltpu.CompilerParams(
            dimension_semantics=("parallel", "arbitrary"), vmem_limit_bytes=VMEM_LIMIT_BYTES),
        name="ffn",
    )(x2d, gpre, w_in, w_in, wdown, gpost)


def _prenorm(x, g):
    step = x.shape[0] // MIX_ROW_SPLIT
    pieces = [_rms(x[i * step:(i + 1) * step, :], g).astype(BF16) for i in range(MIX_ROW_SPLIT)]
    return jnp.concatenate(pieces, axis=0), pieces


def _rows_dot(pieces, w):
    return jnp.concatenate([_dot(p, w) for p in pieces], axis=0)


def _branch_a(h, h_pieces, wcat_ref, lng_ref, lnb_ref, mixm_ref, mixb_ref, outa_scr, vn_out_ref=None):
    rows = h.shape[0]
    a_v = _gelu(_rows_dot(h_pieces, wcat_ref[:, OFF_AV:OFF_AV + 512]))
    mu = jnp.mean(a_v, axis=-1, keepdims=True)
    xc = a_v - mu
    var = jnp.mean(xc * xc, axis=-1, keepdims=True)
    v_n = xc * lax.rsqrt(var + EPS) * lng_ref[...] + lnb_ref[...]
    if vn_out_ref is not None:
        vn_out_ref[...] = v_n
    vb = v_n.astype(BF16)
    a_u = _gelu(_dot(h, wcat_ref[:, OFF_AU:OFF_AU + 512]))
    for g in range(N_GROUPS):
        cols = slice(g * GROUP_W, (g + 1) * GROUP_W)
        mix_g = mixm_ref[g]
        bias_g = mixb_ref[g]
        for s in range(rows // BLOCK_ROWS):
            rws = slice(s * BLOCK_ROWS, (s + 1) * BLOCK_ROWS)
            mixed = _dot(mix_g, vb[rws, cols]) + bias_g
            outa_scr[rws, cols] = (a_u[rws, cols] * mixed).astype(BF16)


def _branch_b(b_in, poolw_ref, pscale_ref, hist_in, hist_out, pooled_scr, outb_scr, *, rs, pos1):
    rows = b_in.shape[0]
    hist_rows = POOL_HIST * rs
    for s in range(rows // BLOCK_ROWS):
        rws = slice(s * BLOCK_ROWS, (s + 1) * BLOCK_ROWS)
        cur = b_in[rws, :]
        ext = jnp.concatenate([hist_in(s), cur], axis=0)
        for g, w in enumerate(POOL_WINDOWS):
            cols = slice(g * GROUP_W, (g + 1) * GROUP_W)
            acc = ext[:, cols]
            sh = 1
            while sh < w:
                acc = acc + pltpu.roll(acc, sh * rs, axis=0)
                sh *= 2
            win = acc[hist_rows:, :]
            if pos1 is None:
                mean = win * (1.0 / w)
            else:
                mean = win / jnp.minimum(pos1, w).astype(F32)
            pooled_scr[rws, cols] = (mean - cur[:, cols]).astype(BF16)
        hist_out(s, ext[BLOCK_ROWS:, :])
    for g in range(N_GROUPS):
        cols = slice(g * GROUP_W, (g + 1) * GROUP_W)
        mixed = _dot(pooled_scr[:, cols], poolw_ref[g]) * pscale_ref[:, cols]
        outb_scr[:, cols] = mixed.astype(BF16)


def _gate_weight(j, width, wcat_ref, wg_ref):
    lo = j * width
    if lo < 3 * D_MODEL:
        return wg_ref[:, lo:lo + width]
    lo -= 3 * D_MODEL
    return wcat_ref[:, OFF_O + lo:OFF_O + lo + width]


def _merge(x, gate, hc, pa_ref, pb_ref, pc_ref, wout_ref, gpost_ref, outa_scr, outb_scr):
    merged = gate(0) * _dot(outa_scr[...], pa_ref[...])
    merged += gate(1) * _dot(outb_scr[...], pb_ref[...])
    out_c = (gate(3) * hc).astype(BF16)
    merged += gate(2) * _dot(out_c, pc_ref[...])
    mb = merged.astype(BF16)
    step = x.shape[0] // MIX_ROW_SPLIT
    outs = []
    for i in range(MIX_ROW_SPLIT):
        rws = slice(i * step, (i + 1) * step)
        outs.append(x[rws, :] + _rms(_dot(mb[rws, :], wout_ref[...]), gpost_ref[...]))
    return jnp.concatenate(outs, axis=0)


def _scan_lanes(x, op, fill):
    lane = lax.broadcasted_iota(jnp.int32, x.shape, 1)
    sh = 1
    while sh < x.shape[1]:
        x = op(x, jnp.where(lane >= sh, pltpu.roll(x, sh, axis=1), fill))
        sh *= 2
    return x


def _last_lane(x):
    lane = lax.broadcasted_iota(jnp.int32, x.shape, 1)
    return jnp.max(jnp.where(lane == x.shape[1] - 1, x, -jnp.inf), axis=1, keepdims=True)


def _mlstm_gates(h_pieces, wift_ref, gbias_ref, m_ref, *, nseq):
    t_len = BLOCK_ROWS
    grow = jnp.concatenate([_dot_nt(wift_ref[...], p) for p in h_pieces], axis=1)
    seqs = range(nseq)
    ig = jnp.concatenate([grow[0:16, s * t_len:(s + 1) * t_len] + gbias_ref[0:16, :] for s in seqs], axis=0)
    lf = _log_sigmoid(
        jnp.concatenate([grow[16:32, s * t_len:(s + 1) * t_len] + gbias_ref[16:32, :] for s in seqs], axis=0))
    m0 = m_ref[...].reshape(nseq * 16, t_len)
    b = _scan_lanes(lf, jnp.add, 0.0)
    a = ig - b
    m_t = b + jnp.maximum(m0, _scan_lanes(a, jnp.maximum, -jnp.inf))
    inter = jnp.exp(b + m0 - m_t)
    b_last = _last_lane(b)
    m_end = _last_lane(m_t)
    decay = jnp.exp(b_last + m0 - m_end)
    w_in = jnp.exp(a + b_last - m_end)
    m_ref[...] = jnp.broadcast_to(m_end, (nseq * 16, t_len)).reshape(nseq, 16, t_len)
    stacked = jnp.concatenate([b - m_t, inter, jnp.exp(-m_t), w_in], axis=0)
    pad = (-stacked.shape[0]) % t_len
    if pad:
        stacked = jnp.concatenate([stacked, jnp.zeros((pad, t_len), F32)], axis=0)
    cols_t = [stacked[i * t_len:(i + 1) * t_len, :].T for i in range(stacked.shape[0] // t_len)]

    def col(quantity, s, hd):
        idx = quantity * nseq * 16 + s * 16 + hd
        return cols_t[idx // t_len][:, idx % t_len:idx % t_len + 1]

    return a, decay, col


def _qkv(h, wcat_ref):
    q = _dot(h, wcat_ref[:, OFF_Q:OFF_Q + 512])
    k = _dot(h, wcat_ref[:, OFF_K:OFF_K + 512]) * (DQK ** -0.5)
    v = _dot(h, wcat_ref[:, OFF_V:OFF_V + D_MODEL])
    return q, k, v


def _mlstm_prompt(qkv, gates, normg_ref, c_ref, n_ref, hc_scr, *, nseq, fillers):
    t_len = BLOCK_ROWS
    a_all, decay_all, col = gates
    q, k, v = qkv
    ri = lax.broadcasted_iota(jnp.int32, (t_len, t_len), 0)
    ci = lax.broadcasted_iota(jnp.int32, (t_len, t_len), 1)
    causal = ci <= ri
    for s in range(nseq):
        rws = slice(s * t_len, (s + 1) * t_len)
        a = a_all[s * 16:(s + 1) * 16, :]
        decay = decay_all[s * 16:(s + 1) * 16, :]
        for hd in range(HEADS):
            fillers[s * HEADS + hd]()
            qh = q[rws, hd * DQK:(hd + 1) * DQK]
            kh = k[rws, hd * DQK:(hd + 1) * DQK]
            vh = v[rws, hd * DV:(hd + 1) * DV]
            qb = qh.astype(BF16)
            kb = kh.astype(BF16)
            bm_c = col(0, s, hd)
            inter_c = col(1, s, hd)
            en_c = col(2, s, hd)
            win_c = col(3, s, hd)
            dmat = jnp.exp(jnp.where(causal, bm_c + a[hd:hd + 1, :], -jnp.inf))
            wmat = _dot_nt(qb, kb) * dmat
            c_old = c_ref[s, hd]
            n_old = n_ref[s, hd:hd + 1, :]
            num = _dot(wmat.astype(BF16), vh.astype(BF16)) + inter_c * _dot_nt(qb, c_old.astype(BF16))
            den = jnp.sum(wmat, axis=1, keepdims=True) + inter_c * jnp.sum(qh * n_old, axis=1, keepdims=True)
            hout = num / jnp.maximum(jnp.abs(den), en_c)
            hc_scr[rws, hd * DV:(hd + 1) * DV] = _rms(hout, normg_ref[:, hd * DV:(hd + 1) * DV])
            dec_row = decay[hd:hd + 1, :]
            c_ref[s, hd] = dec_row * c_old + _dot_tn((vh * win_c).astype(BF16), kb)
            n_ref[s, hd:hd + 1, :] = dec_row * n_old + jnp.sum(kh * win_c, axis=0, keepdims=True)


def _mix_prompt_kernel(x_ref, xnext_ref, gpre_ref, wcat_ref, wg_ref, wift_ref, gbias_ref, lng_ref, lnb_ref,
                       mixm_ref, mixb_ref, poolw_ref, pscale_ref, normg_ref, pa_ref, pb_ref, pc_ref, wout_ref,
                       gpost_ref,
                       o_ref, pool_ref, c_ref, n_ref, m_ref,
                       outa_scr, outb_scr, pooled_scr, hc_scr, gate_scr, h_scr, *, nseq):
    c = pl.program_id(1)
    rows = nseq * BLOCK_ROWS

    @pl.when(c == 0)
    def _():
        pool_ref[...] = jnp.zeros(pool_ref.shape, F32)
        c_ref[...] = jnp.zeros(c_ref.shape, F32)
        n_ref[...] = jnp.zeros(n_ref.shape, F32)
        m_ref[...] = jnp.zeros(m_ref.shape, F32)

    slot = c % 2

    @pl.when(c == 0)
    def _():
        h_scr[0] = _rms(x_ref[...].reshape(rows, D_MODEL), gpre_ref[...]).astype(BF16)

    x = x_ref[...].reshape(rows, D_MODEL)
    h = h_scr[slot]

    gates = _mlstm_gates([h], wift_ref, gbias_ref, m_ref, nseq=nseq)

    _branch_a(h, [h], wcat_ref, lng_ref, lnb_ref, mixm_ref, mixb_ref, outa_scr)

    b_in = _dot(h, wcat_ref[:, OFF_B:OFF_B + 512])
    qkv = _qkv(h, wcat_ref)
    h_scr[1 - slot] = _rms(xnext_ref[...].reshape(rows, D_MODEL), gpre_ref[...]).astype(BF16)

    pos1 = c * BLOCK_ROWS + 1 + lax.broadcasted_iota(jnp.int32, (BLOCK_ROWS, GROUP_W), 0)

    def hist_out(s, val):
        pool_ref[s] = val

    _branch_b(b_in, poolw_ref, pscale_ref, lambda s: pool_ref[s], hist_out, pooled_scr, outb_scr,
              rs=1, pos1=pos1)

    n_fill = nseq * HEADS
    fill_w = 4 * D_MODEL // n_fill

    def make_filler(j):
        def fill():
            gate_scr[:, j * fill_w:(j + 1) * fill_w] = _sigmoid(_dot(h, _gate_weight(j, fill_w, wcat_ref, wg_ref)))
        return fill

    _mlstm_prompt(qkv, gates, normg_ref, c_ref, n_ref, hc_scr, nseq=nseq,
                  fillers=[make_filler(j) for j in range(n_fill)])

    y = _merge(x, lambda j: gate_scr[:, j * D_MODEL:(j + 1) * D_MODEL], hc_scr[...],
               pa_ref, pb_ref, pc_ref, wout_ref, gpost_ref, outa_scr, outb_scr)
    o_ref[...] = y.reshape(nseq, BLOCK_ROWS, D_MODEL)


def _mix_prompt(x3d, lw, layer):
    nb, seq, _ = x3d.shape
    nseq = PROMPT_SEQS_PER_STEP
    rows = nseq * BLOCK_ROWS
    nchunks = seq // BLOCK_ROWS
    cs = functools.partial(_const_spec, index=layer, ngrid=2)
    weights = [lw['mix_norm_pre'], lw['wcat'], lw['wg'], lw['wift'], lw['gbias'], lw['ln_g'], lw['ln_b'],
               lw['mixm_p'], lw['mixb_p'], lw['pool_w'], lw['pool_scale'], lw['norm_g'],
               lw['proj_a'], lw['proj_b'], lw['proj_c'], lw['w_out'], lw['mix_norm_post']]
    out_shapes = (
        jax.ShapeDtypeStruct((nb, seq, D_MODEL), F32),
        jax.ShapeDtypeStruct((nb, POOL_HIST, 512), F32),
        jax.ShapeDtypeStruct((nb, HEADS, DV, DQK), F32),
        jax.ShapeDtypeStruct((nb, 8, DQK), F32),
        jax.ShapeDtypeStruct((nb, 16, BLOCK_ROWS), F32),
    )
    out_specs = (
        pl.BlockSpec((nseq, BLOCK_ROWS, D_MODEL), lambda g, c: (g, c, 0)),
        pl.BlockSpec((nseq, POOL_HIST, 512), lambda g, c: (g, 0, 0)),
        pl.BlockSpec((nseq, HEADS, DV, DQK), lambda g, c: (g, 0, 0, 0)),
        pl.BlockSpec((nseq, 8, DQK), lambda g, c: (g, 0, 0)),
        pl.BlockSpec((nseq, 16, BLOCK_ROWS), lambda g, c: (g, 0, 0)),
    )
    return pl.pallas_call(
        functools.partial(_mix_prompt_kernel, nseq=nseq),
        grid=(nb // nseq, nchunks),
        in_specs=[pl.BlockSpec((nseq, BLOCK_ROWS, D_MODEL), lambda g, c: (g, c, 0)),
                  pl.BlockSpec((nseq, BLOCK_ROWS, D_MODEL), lambda g, c: (g, jnp.minimum(c + 1, nchunks - 1), 0))]
        + [cs(w.shape) for w in weights],
        out_specs=out_specs,
        out_shape=out_shapes,
        scratch_shapes=[pltpu.VMEM((rows, 512), BF16), pltpu.VMEM((rows, 512), BF16),
                        pltpu.VMEM((rows, 512), BF16), pltpu.VMEM((rows, D_MODEL), F32),
                        pltpu.VMEM((rows, 4 * D_MODEL), F32), pltpu.VMEM((2, rows, D_MODEL), BF16)],
        compiler_params=pltpu.CompilerParams(
            dimension_semantics=("parallel", "arbitrary"), vmem_limit_bytes=VMEM_LIMIT_BYTES),
        name="mix_prompt",
    )(x3d, x3d, *weights)


def _mix_s1_kernel(x_ref, hist_ref, gpre_ref, wcat_ref, wifc_ref, lng_ref, lnb_ref, mixm_ref, mixb_ref,
                   poolw_ref, pscale_ref,
                   outa_ref, outb_ref, q_ref, k_ref, v_ref, gi_ref, gf_ref, vn_ref, pool_ref,
                   pooled_scr):
    h, h_pieces = _prenorm(x_ref[...], gpre_ref[...])
    _branch_a(h, h_pieces, wcat_ref, lng_ref, lnb_ref, mixm_ref, mixb_ref, outa_ref, vn_out_ref=vn_ref)

    def hist_out(s, val):
        pool_ref[s] = val

    b_in = _dot(h, wcat_ref[:, OFF_B:OFF_B + 512])
    q, k, v = _qkv(h, wcat_ref)
    _branch_b(b_in, poolw_ref, pscale_ref, lambda s: hist_ref[s], hist_out, pooled_scr, outb_ref,
              rs=SAMPLE_SEQS_PER_BLOCK, pos1=None)
    for j in range(512 // 128):
        q_ref[j] = q[:, j * 128:(j + 1) * 128]
        k_ref[j] = k[:, j * 128:(j + 1) * 128]
    for j in range(D_MODEL // 128):
        v_ref[j] = v[:, j * 128:(j + 1) * 128]
    gi_ref[...] = _dot(h, wifc_ref[:, 0:HEADS * DQK])
    gf_ref[...] = _dot(h, wifc_ref[:, HEADS * DQK:2 * HEADS * DQK])


def _mix_s2_kernel(q_ref, k_ref, v_ref, gi_ref, gf_ref, ib_ref, fb_ref, normg_ref, c_ref, n_ref, m_ref,
                   cbuf_ref, hc_ref, cn_ref, nn_ref, mn_ref,
                   cq_scr, win_scr, dec_scr):
    del cbuf_ref
    nb = SAMPLE_SEQS_PER_BLOCK
    steps = SAMPLE_STEPS

    vblocks = DV // 128

    def slab(ref, t):
        return ref[t * nb:(t + 1) * nb, :]

    def slab_v(ref, hd, t):
        return jnp.concatenate([ref[hd * vblocks + j, t * nb:(t + 1) * nb, :] for j in range(vblocks)], axis=1)

    def twice(x):
        return jnp.concatenate([x] * vblocks, axis=1)

    m0 = m_ref[...]
    ig = [slab(gi_ref, t) + ib_ref[...] for t in range(steps)]
    lf = [_log_sigmoid(slab(gf_ref, t) + fb_ref[...]) for t in range(steps)]
    b, a, m_t = [], [], []
    run_b = jnp.zeros_like(m0)
    run_max = jnp.full(m0.shape, -jnp.inf, F32)
    for t in range(steps):
        run_b = run_b + lf[t]
        b.append(run_b)
        a.append(ig[t] - run_b)
        run_max = jnp.maximum(run_max, a[t])
        m_t.append(run_b + jnp.maximum(m0, run_max))
    inter = [jnp.exp(b[t] + m0 - m_t[t]) for t in range(steps)]
    m_end = m_t[-1]
    decay = jnp.exp(b[-1] + m0 - m_end)
    w_in = [jnp.exp(a[t] + b[-1] - m_end) for t in range(steps)]
    mn_ref[...] = m_end
    for hd in range(HEADS):
        hcols = slice(hd * DQK, (hd + 1) * DQK)
        for t in range(steps):
            win_scr[hd, t * nb:(t + 1) * nb, :] = w_in[t][:, hcols]
            dec_scr[hd, t * nb:(t + 1) * nb, :] = decay[:, hcols]

    def seq_body(bi, carry):
        rows = pl.ds(bi, steps, stride=nb)
        for hd in range(HEADS):
            c_old = c_ref[bi, hd]
            qb = q_ref[hd, rows, :].astype(BF16)
            cq = _dot_nt(qb, c_old.astype(BF16))
            for j in range(vblocks):
                cq_scr[hd * vblocks + j, rows, :] = cq[:, j * 128:(j + 1) * 128]
            vrows = jnp.concatenate([v_ref[hd * vblocks + j, rows, :] for j in range(vblocks)], axis=1)
            vs = (vrows * twice(win_scr[hd, rows, :])).astype(BF16)
            kb = k_ref[hd, rows, :].astype(BF16)
            dec_row = dec_scr[hd, rows, :][0:1, :]
            cn_ref[bi, hd] = dec_row * c_old + _dot_tn(vs, kb)
        return carry

    lax.fori_loop(0, nb, seq_body, 0, unroll=SAMPLE_SEQ_UNROLL)

    pairs = [(t, s) for t in range(steps) for s in range(t + 1)]
    dfac = [jnp.exp((b[t] - m_t[t]) + a[s]) for (t, s) in pairs]
    enm = [jnp.exp(-m_t[t]) for t in range(steps)]
    for hd in range(HEADS):
        hcols = slice(hd * DQK, (hd + 1) * DQK)
        vcols = slice(hd * DV, (hd + 1) * DV)
        qs = [slab(q_ref.at[hd], t) for t in range(steps)]
        ks = [slab(k_ref.at[hd], t) for t in range(steps)]
        vs = [slab_v(v_ref, hd, t) for t in range(steps)]
        n_old = n_ref[:, hcols]
        n_new = decay[:, hcols] * n_old
        for t in range(steps):
            n_new = n_new + w_in[t][:, hcols] * ks[t]
        nn_ref[:, hcols] = n_new
        prods = jnp.concatenate([qs[t] * ks[s] for (t, s) in pairs] + [qs[t] * n_old for t in range(steps)], axis=0)
        red = jnp.sum(prods, axis=1, keepdims=True)
        wts = red[:len(pairs) * nb] * jnp.concatenate([d[:, hcols] for d in dfac], axis=0)
        for t in range(steps):
            inter_t = inter[t][:, hcols]
            num = twice(inter_t) * slab_v(cq_scr, hd, t)
            den = inter_t * red[(len(pairs) + t) * nb:(len(pairs) + t + 1) * nb]
            for s in range(t + 1):
                p = pairs.index((t, s))
                w_ts = wts[p * nb:(p + 1) * nb]
                num = num + twice(w_ts) * vs[s]
                den = den + w_ts
            hout = num / twice(jnp.maximum(jnp.abs(den), enm[t][:, hcols]))
            hc_ref[t * nb:(t + 1) * nb, vcols] = _rms(hout, normg_ref[:, vcols])


def _mix_s3_kernel(x_ref, outa_ref, outb_ref, hc_ref, gpre_ref, wcat_ref, wg_ref, pa_ref, pb_ref, pc_ref,
                   wout_ref, gpost_ref, o_ref):
    x = x_ref[...]
    h, h_pieces = _prenorm(x, gpre_ref[...])

    def gate(j):
        pieces = h_pieces if j == 0 else [h]
        return _sigmoid(_rows_dot(pieces, _gate_weight(j, D_MODEL, wcat_ref, wg_ref)))

    o_ref[...] = _merge(x, gate, hc_ref[...], pa_ref, pb_ref, pc_ref, wout_ref, gpost_ref, outa_ref, outb_ref)


def _mix_sample(xs, hist, c_all, c_buf, n_in, m_in, lw, layer):
    rows = xs.shape[0]
    nblk = rows // BLOCK_ROWS
    hist_rows = POOL_HIST * SAMPLE_SEQS_PER_BLOCK
    cs1 = functools.partial(_const_spec, index=layer, ngrid=1)
    full = lambda shape: pl.BlockSpec(shape, lambda i: (0,) * len(shape))
    params = pltpu.CompilerParams(dimension_semantics=("arbitrary",), vmem_limit_bytes=VMEM_LIMIT_BYTES)

    w1 = [lw['mix_norm_pre'], lw['wcat'], lw['wifc'], lw['ln_g'], lw['ln_b'], lw['mixm_s'], lw['mixb_s'],
          lw['pool_w'], lw['pool_scale']]
    s1_shapes = (
        jax.ShapeDtypeStruct((rows, 512), BF16),
        jax.ShapeDtypeStruct((rows, 512), BF16),
        jax.ShapeDtypeStruct((HEADS, rows, DQK), F32),
        jax.ShapeDtypeStruct((HEADS, rows, DQK), F32),
        jax.ShapeDtypeStruct((D_MODEL // 128, rows, 128), F32),
        jax.ShapeDtypeStruct((rows, HEADS * DQK), F32),
        jax.ShapeDtypeStruct((rows, HEADS * DQK), F32),
        jax.ShapeDtypeStruct((rows, 512), F32),
        jax.ShapeDtypeStruct((nblk, hist_rows, 512), F32),
    )
    outa, outb, q, k, v, gi, gf, vn, pool_new = pl.pallas_call(
        _mix_s1_kernel,
        grid=(1,),
        in_specs=[full(xs.shape), full(hist.shape)] + [cs1(w.shape) for w in w1],
        out_specs=tuple(full(s.shape) for s in s1_shapes),
        out_shape=s1_shapes,
        scratch_shapes=[pltpu.VMEM((rows, 512), BF16)],
        compiler_params=params,
        name="mix_s1",
    )(xs, hist, *w1)

    nb = SAMPLE_SEQS_PER_BLOCK
    blk = lambda width: pl.BlockSpec((BLOCK_ROWS, width), lambda j: (j, 0))
    seqblk = lambda width: pl.BlockSpec((nb, width), lambda j: (j, 0))
    cblk = pl.BlockSpec((None, nb, HEADS, DV, DQK), lambda j: (layer, j, 0, 0, 0))
    lrow = lambda width: pl.BlockSpec((None, 1, width), lambda j: (layer, 0, 0))
    colblk = lambda n: pl.BlockSpec((n, BLOCK_ROWS, 128), lambda j: (0, j, 0))
    hc, c_buf, n_new, m_new = pl.pallas_call(
        _mix_s2_kernel,
        grid=(nblk,),
        in_specs=[colblk(HEADS), colblk(HEADS), colblk(D_MODEL // 128), blk(512), blk(512),
                  lrow(512), lrow(512), lrow(D_MODEL), cblk, seqblk(512), seqblk(512),
                  pl.BlockSpec(memory_space=pl.ANY)],
        out_specs=(blk(D_MODEL), cblk, seqblk(512), seqblk(512)),
        out_shape=(jax.ShapeDtypeStruct((rows, D_MODEL), F32),
                   jax.ShapeDtypeStruct(c_all.shape, F32),
                   jax.ShapeDtypeStruct(n_in.shape, F32),
                   jax.ShapeDtypeStruct(m_in.shape, F32)),
        input_output_aliases={11: 1},
        scratch_shapes=[pltpu.VMEM((D_MODEL // 128, BLOCK_ROWS, 128), F32),
                        pltpu.VMEM((HEADS, BLOCK_ROWS, 128), F32), pltpu.VMEM((HEADS, BLOCK_ROWS, 128), F32)],
        compiler_params=pltpu.CompilerParams(dimension_semantics=("parallel",),
                                             vmem_limit_bytes=VMEM_LIMIT_BYTES),
        name="mix_s2",
    )(q, k, v, gi, gf, lw['ib'], lw['fb'], lw['norm_g'], c_all, n_in, m_in, c_buf)

    w3 = [lw['mix_norm_pre'], lw['wcat'], lw['wg'], lw['proj_a'], lw['proj_b'], lw['proj_c'], lw['w_out'],
          lw['mix_norm_post']]
    y = pl.pallas_call(
        _mix_s3_kernel,
        grid=(1,),
        in_specs=[full(xs.shape), full(outa.shape), full(outb.shape), full(hc.shape)]
        + [cs1(w.shape) for w in w3],
        out_specs=full(xs.shape),
        out_shape=jax.ShapeDtypeStruct(xs.shape, F32),
        compiler_params=params,
        name="mix_s3",
    )(xs, outa, outb, hc, *w3)
    return y, vn, pool_new, c_buf, n_new, m_new


def _to_blocked(x):
    nseq, t, c = x.shape
    nb = SAMPLE_SEQS_PER_BLOCK
    return x.reshape(nseq // nb, nb, t, c).transpose(0, 2, 1, 3).reshape(nseq * t, c)


def _from_blocked(x, t):
    rows, c = x.shape
    nb = SAMPLE_SEQS_PER_BLOCK
    nseq = rows // t
    return x.reshape(nseq // nb, t, nb, c).transpose(0, 2, 1, 3).reshape(nseq, t, c)


def _split_w_in_kernel(w_ref, tail_ref, wcat_ref, wg_ref):
    wcat_ref[...] = w_ref[:, 0:WCAT_COLS].astype(BF16)
    nblk = 3 * D_MODEL // 128
    lane = lax.broadcasted_iota(jnp.int32, (w_ref.shape[0], 128), 1)
    for j in range(nblk):
        lo = WCAT_COLS + j * 128
        cur = w_ref[:, lo:lo + 128]
        nxt = w_ref[:, lo + 128:lo + 256] if j + 1 < nblk else tail_ref[...]
        out = jnp.where(lane < 128 - GATE_COLS,
                        pltpu.roll(cur, 128 - GATE_COLS, axis=1), pltpu.roll(nxt, 128 - GATE_COLS, axis=1))
        wg_ref[:, j * 128:(j + 1) * 128] = out.astype(BF16)


def _split_w_in(w_in):
    in_cols = w_in.shape[2]
    rows = 256
    tail = jnp.pad(w_in[:, :, in_cols - GATE_COLS:], ((0, 0), (0, 0), (0, 128 - GATE_COLS)))
    return pl.pallas_call(
        _split_w_in_kernel,
        grid=(DEPTH, D_MODEL // rows),
        in_specs=[pl.BlockSpec((None, rows, in_cols), lambda l, r: (l, r, 0)),
                  pl.BlockSpec((None, rows, 128), lambda l, r: (l, r, 0))],
        out_specs=(pl.BlockSpec((None, rows, WCAT_COLS), lambda l, r: (l, r, 0)),
                   pl.BlockSpec((None, rows, 3 * D_MODEL), lambda l, r: (l, r, 0))),
        out_shape=(jax.ShapeDtypeStruct((DEPTH, D_MODEL, WCAT_COLS), BF16),
                   jax.ShapeDtypeStruct((DEPTH, D_MODEL, 3 * D_MODEL), BF16)),
        compiler_params=pltpu.CompilerParams(
            dimension_semantics=("parallel", "parallel"), vmem_limit_bytes=VMEM_LIMIT_BYTES),
        name="split_w_in",
    )(w_in, tail)


def _prep_weights(p):
    w_in = p['w_in']
    row = lambda a: a.reshape(DEPTH, 1, -1).astype(F32)
    wcat, wg = _split_w_in(w_in)
    wi = w_in[:, :, WCAT_COLS:WCAT_COLS + HEADS]
    wf = w_in[:, :, WCAT_COLS + HEADS:WCAT_COLS + 2 * HEADS]
    wift = jnp.zeros((DEPTH, 32, D_MODEL), F32)
    wift = wift.at[:, 0:HEADS].set(wi.transpose(0, 2, 1)).at[:, 16:16 + HEADS].set(wf.transpose(0, 2, 1))
    wifc = jnp.concatenate([jnp.repeat(wi, DQK, axis=2), jnp.repeat(wf, DQK, axis=2)], axis=2)
    gbias = jnp.zeros((DEPTH, 32, BLOCK_ROWS), F32)
    gbias = gbias.at[:, 0:HEADS].set(jnp.broadcast_to(p['mlstm_i_bias'][:, :, None], (DEPTH, HEADS, BLOCK_ROWS)))
    gbias = gbias.at[:, 16:16 + HEADS].set(
        jnp.broadcast_to(p['mlstm_f_bias'][:, :, None], (DEPTH, HEADS, BLOCK_ROWS)))
    lane_pad = lambda a: jnp.repeat(a.astype(F32), DQK, axis=1).reshape(DEPTH, 1, HEADS * DQK)

    ws = p['gmlp_w_s']
    bs = p['gmlp_b_s']
    mixm_p = jnp.tril(ws)
    mixb_p = jnp.broadcast_to(bs[..., None], ws.shape)
    nb = SAMPLE_SEQS_PER_BLOCK
    tril_s = jnp.tril(ws[:, :, :SAMPLE_STEPS, :SAMPLE_STEPS])
    eye = jnp.eye(nb, dtype=F32)
    mixm_s = jnp.einsum('lgts,bc->lgtbsc', tril_s, eye).reshape(DEPTH, N_GROUPS, BLOCK_ROWS, BLOCK_ROWS)
    mixb_s = jnp.broadcast_to(jnp.repeat(bs[:, :, :SAMPLE_STEPS], nb, axis=2)[..., None], ws.shape)

    return dict(
        ffn1_norm_pre=row(p['ffn1_norm_pre']), ffn1_w_in=p['ffn1_w_in'].astype(BF16),
        ffn1_w_down=p['ffn1_w_down'].astype(BF16), ffn1_norm_post=row(p['ffn1_norm_post']),
        ffn2_norm_pre=row(p['ffn2_norm_pre']), ffn2_w_in=p['ffn2_w_in'].astype(BF16),
        ffn2_w_down=p['ffn2_w_down'].astype(BF16), ffn2_norm_post=row(p['ffn2_norm_post']),
        mix_norm_pre=row(p['mix_norm_pre']), mix_norm_post=row(p['mix_norm_post']),
        wcat=wcat, wg=wg,
        wift=wift.astype(BF16), wifc=wifc.astype(BF16), gbias=gbias,
        ib=lane_pad(p['mlstm_i_bias']), fb=lane_pad(p['mlstm_f_bias']),
        ln_g=row(p['gmlp_ln_g']), ln_b=row(p['gmlp_ln_b']),
        mixm_p=mixm_p.astype(BF16), mixb_p=mixb_p.astype(F32),
        mixm_s=mixm_s.astype(BF16), mixb_s=mixb_s.astype(F32),
        pool_w=p['pool_w'].astype(BF16), pool_scale=row(p['pool_scale']), norm_g=row(p['mlstm_norm_g']),
        proj_a=p['proj_a'].astype(BF16), proj_b=p['proj_b'].astype(BF16), proj_c=p['proj_c'].astype(BF16),
        w_out=p['w_out'].astype(BF16),
    )


def kernel(x_prompt, x_sample, state_pool, state_mlstm_C, state_mlstm_n, state_mlstm_m, ffn1_norm_pre, ffn1_w_in, ffn1_w_down, ffn1_norm_post, mix_norm_pre, w_in, gmlp_ln_g, gmlp_ln_b, gmlp_w_s, gmlp_b_s, pool_w, pool_scale, mlstm_i_bias, mlstm_f_bias, mlstm_norm_g, proj_a, proj_b, proj_c, w_out, mix_norm_post, ffn2_norm_pre, ffn2_w_in, ffn2_w_down, ffn2_norm_post):
    params = dict(ffn1_norm_pre=ffn1_norm_pre, ffn1_w_in=ffn1_w_in, ffn1_w_down=ffn1_w_down,
                  ffn1_norm_post=ffn1_norm_post, mix_norm_pre=mix_norm_pre, w_in=w_in, gmlp_ln_g=gmlp_ln_g,
                  gmlp_ln_b=gmlp_ln_b, gmlp_w_s=gmlp_w_s, gmlp_b_s=gmlp_b_s, pool_w=pool_w,
                  pool_scale=pool_scale, mlstm_i_bias=mlstm_i_bias, mlstm_f_bias=mlstm_f_bias,
                  mlstm_norm_g=mlstm_norm_g, proj_a=proj_a, proj_b=proj_b, proj_c=proj_c, w_out=w_out,
                  mix_norm_post=mix_norm_post, ffn2_norm_pre=ffn2_norm_pre, ffn2_w_in=ffn2_w_in,
                  ffn2_w_down=ffn2_w_down, ffn2_norm_post=ffn2_norm_post)
    lw = _prep_weights(params)
    nbp, seq, _ = x_prompt.shape
    nbs, steps, _ = x_sample.shape

    yp = x_prompt
    ys = _to_blocked(x_sample)
    hist_all = jnp.pad(state_pool, ((0, 0), (0, 0), (1, 0), (0, 0)))
    n_all = state_mlstm_n.reshape(DEPTH, nbs, HEADS * DQK)
    m_all = jnp.repeat(state_mlstm_m, DQK, axis=2)

    pool_p, c_p, n_p, m_p = [], [], [], []
    pool_s, n_s, m_s, v_s = [], [], [], []
    c_sample = jnp.zeros(state_mlstm_C.shape, F32)
    for l in range(DEPTH):
        ffn1 = (lw['ffn1_norm_pre'], lw['ffn1_w_in'], lw['ffn1_w_down'], lw['ffn1_norm_post'])
        ffn2 = (lw['ffn2_norm_pre'], lw['ffn2_w_in'], lw['ffn2_w_down'], lw['ffn2_norm_post'])
        yp = _ffn(yp.reshape(nbp * seq, D_MODEL), *ffn1, l).reshape(nbp, seq, D_MODEL)
        yp, pb, cf, nf_, mf = _mix_prompt(yp, lw, l)
        yp = _ffn(yp.reshape(nbp * seq, D_MODEL), *ffn2, l).reshape(nbp, seq, D_MODEL)
        pool_p.append(pb[:, 1:, :])
        c_p.append(cf)
        n_p.append(nf_[:, :HEADS, :])
        m_p.append(mf[:, :HEADS, 0])
        ys = _ffn(ys, *ffn1, l)
        hist = _to_blocked(hist_all[l]).reshape(nbs // SAMPLE_SEQS_PER_BLOCK,
                                                POOL_HIST * SAMPLE_SEQS_PER_BLOCK, 512)
        ys, vn, pool_new, c_sample, n_new, m_new = _mix_sample(
            ys, hist, state_mlstm_C, c_sample, n_all[l], m_all[l], lw, l)
        ys = _ffn(ys, *ffn2, l)
        pool_s.append(_from_blocked(pool_new.reshape(nbs * POOL_HIST, 512), POOL_HIST)[:, 1:, :])
        n_s.append(n_new.reshape(nbs, HEADS, DQK))
        m_s.append(m_new.reshape(nbs, HEADS, DQK)[:, :, 0])
        v_s.append(_from_blocked(vn, steps))
    return (yp, _from_blocked(ys, steps),
            jnp.stack(pool_p), jnp.stack(c_p), jnp.stack(n_p), jnp.stack(m_p),
            jnp.stack(pool_s), c_sample, jnp.stack(n_s), jnp.stack(m_s), jnp.stack(v_s))
```

```python
import functools

import jax
import jax.numpy as jnp
from jax import lax
from jax.experimental import pallas as pl
from jax.experimental.pallas import tpu as pltpu

F32 = jnp.float32
BF16 = jnp.bfloat16

D_MODEL = 1024
D_FF = 2816
DEPTH = 4
EPS = 1e-6
PAST_LEN = 16384
N_GROUPS = 4
GROUP_W = 128
POOL_WINDOWS = (2, 4, 8, 16)
POOL_HIST = 16
HEADS = 4
DQK = 128
DV = 256
BLOCK_ROWS = 128
SAMPLE_SEQS_PER_BLOCK = 16
SAMPLE_STEPS = 8

OFF_AU, OFF_AV, OFF_B, OFF_Q, OFF_K, OFF_V, OFF_O = 0, 512, 1024, 1536, 2048, 2560, 3584
GATE_COLS = 8
OFF_G = 4608
WT_ROWS = OFF_G + 3 * D_MODEL

SAMPLE_SEQ_UNROLL = 4

FFN_TM = 2048
FFN_TF = 256
MIX_ROW_SPLIT = 2
FFN_ROW_BLOCKS = 4
PROMPT_SEQS_PER_STEP = 4
VMEM_LIMIT_BYTES = 56 * 1024 * 1024


def _dot(a, b):
    return jnp.dot(a, b, preferred_element_type=F32)


def _dot_nt(a, b):
    return lax.dot_general(a, b, (((1,), (1,)), ((), ())), preferred_element_type=F32)


def _dot_tn(a, b):
    return lax.dot_general(a, b, (((0,), (0,)), ((), ())), preferred_element_type=F32)


def _rms(x, g):
    return x * lax.rsqrt(jnp.mean(x * x, axis=-1, keepdims=True) + EPS) * g


def _sigmoid(x):
    return 1.0 / (1.0 + jnp.exp(-x))


def _gelu(x):
    return 0.5 * x * (1.0 + jnp.tanh(0.7978845608028654 * (x + 0.044715 * (x * x * x))))


def _log_sigmoid(x):
    return jnp.minimum(x, 0.0) - jnp.log1p(jnp.exp(-jnp.abs(x)))


def _const_spec(shape, index, ngrid):
    zeros = (0,) * (len(shape) - 1)
    if ngrid == 1:
        imap = lambda i: (index,) + zeros
    else:
        imap = lambda i, j: (index,) + zeros
    return pl.BlockSpec((None,) + tuple(shape[1:]), imap, pipeline_mode=pl.Buffered(1))


def _ffn_tiles(m):
    if m > FFN_TM:
        return FFN_TM, FFN_TF, FFN_ROW_BLOCKS
    return m, D_FF // 2, FFN_ROW_BLOCKS // 2


def _ffn_kernel(x_ref, gpre_ref, wgate_ref, wup_ref, wdown_ref, gpost_ref, o_ref, h_scr, *, row_blocks, nf):
    f = pl.program_id(1)
    last = nf - 1
    tm = x_ref.shape[0]
    rb = tm // row_blocks

    def step(first, final):
        wgate = wgate_ref[...]
        wup = wup_ref[...]
        wdown = wdown_ref[...]
        for r in range(row_blocks):
            rows = slice(r * rb, (r + 1) * rb)
            if first:
                h = _rms(x_ref[rows, :], gpre_ref[...]).astype(BF16)
                h_scr[rows, :] = h
            else:
                h = h_scr[rows, :]
            gate = _dot(h, wgate)
            up = _dot(h, wup)
            act = (gate * _sigmoid(gate) * up).astype(BF16)
            part = _dot(act, wdown)
            acc = part if first else o_ref[rows, :] + part
            if final:
                acc = x_ref[rows, :] + _rms(acc, 0.5 * gpost_ref[...])
            o_ref[rows, :] = acc

    @pl.when(f == 0)
    def _():
        step(True, False)

    if nf > 2:
        @pl.when(jnp.logical_and(f > 0, f < last))
        def _():
            step(False, False)

    @pl.when(f == last)
    def _():
        step(False, True)


def _ffn(x2d, gpre, w_in, wdown, gpost, layer):
    m = x2d.shape[0]
    tm, tf, row_blocks = _ffn_tiles(m)
    nf = D_FF // tf
    return pl.pallas_call(
        functools.partial(_ffn_kernel, row_blocks=row_blocks, nf=nf),
        grid=(m // tm, nf),
        in_specs=[
            pl.BlockSpec((tm, D_MODEL), lambda i, f: (i, 0)),
            pl.BlockSpec((None, 1, D_MODEL), lambda i, f: (layer, 0, 0)),
            pl.BlockSpec((None, D_MODEL, tf), lambda i, f: (layer, 0, f)),
            pl.BlockSpec((None, D_MODEL, tf), lambda i, f: (layer, 0, f + nf)),
            pl.BlockSpec((None, tf, D_MODEL), lambda i, f: (layer, f, 0)),
            pl.BlockSpec((None, 1, D_MODEL), lambda i, f: (layer, 0, 0)),
        ],
        out_specs=pl.BlockSpec((tm, D_MODEL), lambda i, f: (i, 0)),
        out_shape=jax.ShapeDtypeStruct((m, D_MODEL), F32),
        scratch_shapes=[pltpu.VMEM((tm, D_MODEL), BF16)],
        compiler_params=pltpu.CompilerParams(
            dimension_semantics=("parallel", "arbitrary"), vmem_limit_bytes=VMEM_LIMIT_BYTES),
        name="ffn",
    )(x2d, gpre, w_in, w_in, wdown, gpost)


def _prenorm(x, g):
    step = x.shape[0] // MIX_ROW_SPLIT
    pieces = [_rms(x[i * step:(i + 1) * step, :], g).astype(BF16) for i in range(MIX_ROW_SPLIT)]
    return jnp.concatenate(pieces, axis=0), pieces


def _proj(h, wt_ref, lo, n):
    return _dot_nt(h, wt_ref[lo:lo + n, :])


def _rows_dot_nt(pieces, wt):
    return jnp.concatenate([_dot_nt(p, wt) for p in pieces], axis=0)


def _branch_a(h, h_pieces, wcat_ref, lng_ref, lnb_ref, mixm_ref, mixb_ref, outa_scr, vn_out_ref=None):
    rows = h.shape[0]
    a_v = _gelu(_rows_dot_nt(h_pieces, wcat_ref[OFF_AV:OFF_AV + 512, :]))
    mu = jnp.mean(a_v, axis=-1, keepdims=True)
    xc = a_v - mu
    var = jnp.mean(xc * xc, axis=-1, keepdims=True)
    v_n = xc * lax.rsqrt(var + EPS) * lng_ref[...] + lnb_ref[...]
    if vn_out_ref is not None:
        vn_out_ref[...] = v_n
    vb = v_n.astype(BF16)
    a_u = _gelu(_proj(h, wcat_ref, OFF_AU, 512))
    for g in range(N_GROUPS):
        cols = slice(g * GROUP_W, (g + 1) * GROUP_W)
        mix_g = mixm_ref[g]
        bias_g = mixb_ref[g]
        for s in range(rows // BLOCK_ROWS):
            rws = slice(s * BLOCK_ROWS, (s + 1) * BLOCK_ROWS)
            mixed = _dot(mix_g, vb[rws, cols]) + bias_g
            outa_scr[rws, cols] = (a_u[rws, cols] * mixed).astype(BF16)


def _branch_b(b_in, poolw_ref, pscale_ref, hist_in, hist_out, pooled_scr, outb_scr, *, rs, pos1):
    rows = b_in.shape[0]
    hist_rows = POOL_HIST * rs
    for s in range(rows // BLOCK_ROWS):
        rws = slice(s * BLOCK_ROWS, (s + 1) * BLOCK_ROWS)
        cur = b_in[rws, :]
        ext = jnp.concatenate([hist_in(s), cur], axis=0)
        for g, w in enumerate(POOL_WINDOWS):
            cols = slice(g * GROUP_W, (g + 1) * GROUP_W)
            acc = ext[:, cols]
            sh = 1
            while sh < w:
                acc = acc + pltpu.roll(acc, sh * rs, axis=0)
                sh *= 2
            win = acc[hist_rows:, :]
            if pos1 is None:
                mean = win * (1.0 / w)
            else:
                mean = win / jnp.minimum(pos1, w).astype(F32)
            pooled_scr[rws, cols] = (mean - cur[:, cols]).astype(BF16)
        hist_out(s, ext[BLOCK_ROWS:, :])
    for g in range(N_GROUPS):
        cols = slice(g * GROUP_W, (g + 1) * GROUP_W)
        mixed = _dot(pooled_scr[:, cols], poolw_ref[g]) * pscale_ref[:, cols]
        outb_scr[:, cols] = mixed.astype(BF16)


def _gate_weight(j, width, wcat_ref, row0=0):
    lo = j * width
    lo = OFF_G + lo if lo < 3 * D_MODEL else OFF_O + lo - 3 * D_MODEL
    return wcat_ref[lo - row0:lo - row0 + width, :]


def _merge(x, gate, hc, pa_ref, pb_ref, pc_ref, wout_ref, gpost_ref, outa_scr, outb_scr):
    merged = gate(0) * _dot(outa_scr[...], pa_ref[...])
    merged += gate(1) * _dot(outb_scr[...], pb_ref[...])
    out_c = (gate(3) * hc).astype(BF16)
    merged += gate(2) * _dot(out_c, pc_ref[...])
    mb = merged.astype(BF16)
    step = x.shape[0] // MIX_ROW_SPLIT
    outs = []
    for i in range(MIX_ROW_SPLIT):
        rws = slice(i * step, (i + 1) * step)
        outs.append(x[rws, :] + _rms(_dot(mb[rws, :], wout_ref[...]), gpost_ref[...]))
    return jnp.concatenate(outs, axis=0)


def _scan_lanes(x, op, fill):
    lane = lax.broadcasted_iota(jnp.int32, x.shape, 1)
    sh = 1
    while sh < x.shape[1]:
        x = op(x, jnp.where(lane >= sh, pltpu.roll(x, sh, axis=1), fill))
        sh *= 2
    return x


def _last_lane(x):
    lane = lax.broadcasted_iota(jnp.int32, x.shape, 1)
    return jnp.max(jnp.where(lane == x.shape[1] - 1, x, -jnp.inf), axis=1, keepdims=True)


def _mlstm_gates(h_pieces, wift_ref, gbias_ref, m_ref, *, nseq):
    t_len = BLOCK_ROWS
    grow = jnp.concatenate([_dot_nt(wift_ref[...], p) for p in h_pieces], axis=1)
    seqs = range(nseq)
    ig = jnp.concatenate([grow[0:16, s * t_len:(s + 1) * t_len] + gbias_ref[0:16, :] for s in seqs], axis=0)
    lf = _log_sigmoid(
        jnp.concatenate([grow[16:32, s * t_len:(s + 1) * t_len] + gbias_ref[16:32, :] for s in seqs], axis=0))
    m0 = m_ref[...].reshape(nseq * 16, t_len)
    b = _scan_lanes(lf, jnp.add, 0.0)
    a = ig - b
    m_t = b + jnp.maximum(m0, _scan_lanes(a, jnp.maximum, -jnp.inf))
    inter = jnp.exp(b + m0 - m_t)
    b_last = _last_lane(b)
    m_end = _last_lane(m_t)
    decay = jnp.exp(b_last + m0 - m_end)
    w_in = jnp.exp(a + b_last - m_end)
    m_ref[...] = jnp.broadcast_to(m_end, (nseq * 16, t_len)).reshape(nseq, 16, t_len)
    stacked = jnp.concatenate([b - m_t, inter, jnp.exp(-m_t), w_in], axis=0)
    pad = (-stacked.shape[0]) % t_len
    if pad:
        stacked = jnp.concatenate([stacked, jnp.zeros((pad, t_len), F32)], axis=0)
    cols_t = [stacked[i * t_len:(i + 1) * t_len, :].T for i in range(stacked.shape[0] // t_len)]

    def col(quantity, s, hd):
        idx = quantity * nseq * 16 + s * 16 + hd
        return cols_t[idx // t_len][:, idx % t_len:idx % t_len + 1]

    return a, decay, col


def _qkv(h, wcat_ref):
    q = _proj(h, wcat_ref, OFF_Q, 512)
    k = _proj(h, wcat_ref, OFF_K, 512) * (DQK ** -0.5)
    v = _proj(h, wcat_ref, OFF_V, D_MODEL)
    return q, k, v


def _mlstm_prompt(qkv, gates, normg_ref, c_ref, n_ref, hc_scr, *, nseq, fillers):
    t_len = BLOCK_ROWS
    a_all, decay_all, col = gates
    q, k, v = qkv
    ri = lax.broadcasted_iota(jnp.int32, (t_len, t_len), 0)
    ci = lax.broadcasted_iota(jnp.int32, (t_len, t_len), 1)
    causal = ci <= ri
    for s in range(nseq):
        rws = slice(s * t_len, (s + 1) * t_len)
        a = a_all[s * 16:(s + 1) * 16, :]
        decay = decay_all[s * 16:(s + 1) * 16, :]
        for hd in range(HEADS):
            fillers[s * HEADS + hd]()
            qh = q[rws, hd * DQK:(hd + 1) * DQK]
            kh = k[rws, hd * DQK:(hd + 1) * DQK]
            vh = v[rws, hd * DV:(hd + 1) * DV]
            qb = qh.astype(BF16)
            kb = kh.astype(BF16)
            bm_c = col(0, s, hd)
            inter_c = col(1, s, hd)
            en_c = col(2, s, hd)
            win_c = col(3, s, hd)
            dmat = jnp.exp(jnp.where(causal, bm_c + a[hd:hd + 1, :], -jnp.inf))
            wmat = _dot_nt(qb, kb) * dmat
            c_old = c_ref[s, hd]
            n_old = n_ref[s, hd:hd + 1, :]
            num = _dot(wmat.astype(BF16), vh.astype(BF16)) + inter_c * _dot_nt(qb, c_old.astype(BF16))
            den = jnp.sum(wmat, axis=1, keepdims=True) + inter_c * jnp.sum(qh * n_old, axis=1, keepdims=True)
            hout = num / jnp.maximum(jnp.abs(den), en_c)
            hc_scr[rws, hd * DV:(hd + 1) * DV] = _rms(hout, normg_ref[:, hd * DV:(hd + 1) * DV])
            dec_row = decay[hd:hd + 1, :]
            c_ref[s, hd] = dec_row * c_old + _dot_tn((vh * win_c).astype(BF16), kb)
            n_ref[s, hd:hd + 1, :] = dec_row * n_old + jnp.sum(kh * win_c, axis=0, keepdims=True)


def _mix_prompt_kernel(x_ref, xnext_ref, gpre_ref, wcat_ref, wift_ref, gbias_ref, lng_ref, lnb_ref,
                       mixm_ref, mixb_ref, poolw_ref, pscale_ref, normg_ref, pa_ref, pb_ref, pc_ref, wout_ref,
                       gpost_ref,
                       o_ref, pool_ref, c_ref, n_ref, m_ref,
                       outa_scr, outb_scr, pooled_scr, hc_scr, gate_scr, h_scr, *, nseq):
    c = pl.program_id(1)
    rows = nseq * BLOCK_ROWS

    @pl.when(c == 0)
    def _():
        pool_ref[...] = jnp.zeros(pool_ref.shape, F32)
        c_ref[...] = jnp.zeros(c_ref.shape, F32)
        n_ref[...] = jnp.zeros(n_ref.shape, F32)
        m_ref[...] = jnp.zeros(m_ref.shape, F32)

    slot = c % 2

    @pl.when(c == 0)
    def _():
        h_scr[0] = _rms(x_ref[...].reshape(rows, D_MODEL), gpre_ref[...]).astype(BF16)

    x = x_ref[...].reshape(rows, D_MODEL)
    h = h_scr[slot]

    gates = _mlstm_gates([h], wift_ref, gbias_ref, m_ref, nseq=nseq)

    _branch_a(h, [h], wcat_ref, lng_ref, lnb_ref, mixm_ref, mixb_ref, outa_scr)

    b_in = _proj(h, wcat_ref, OFF_B, 512)
    qkv = _qkv(h, wcat_ref)
    h_scr[1 - slot] = _rms(xnext_ref[...].reshape(rows, D_MODEL), gpre_ref[...]).astype(BF16)

    pos1 = c * BLOCK_ROWS + 1 + lax.broadcasted_iota(jnp.int32, (BLOCK_ROWS, GROUP_W), 0)

    def hist_out(s, val):
        pool_ref[s] = val

    _branch_b(b_in, poolw_ref, pscale_ref, lambda s: pool_ref[s], hist_out, pooled_scr, outb_scr,
              rs=1, pos1=pos1)

    n_fill = nseq * HEADS
    fill_w = 4 * D_MODEL // n_fill

    def make_filler(j):
        def fill():
            gate_scr[:, j * fill_w:(j + 1) * fill_w] = _sigmoid(_dot_nt(h, _gate_weight(j, fill_w, wcat_ref)))
        return fill

    _mlstm_prompt(qkv, gates, normg_ref, c_ref, n_ref, hc_scr, nseq=nseq,
                  fillers=[make_filler(j) for j in range(n_fill)])

    y = _merge(x, lambda j: gate_scr[:, j * D_MODEL:(j + 1) * D_MODEL], hc_scr[...],
               pa_ref, pb_ref, pc_ref, wout_ref, gpost_ref, outa_scr, outb_scr)
    o_ref[...] = y.reshape(nseq, BLOCK_ROWS, D_MODEL)


def _mix_prompt(x3d, lw, layer):
    nb, seq, _ = x3d.shape
    nseq = PROMPT_SEQS_PER_STEP
    rows = nseq * BLOCK_ROWS
    nchunks = seq // BLOCK_ROWS
    cs = functools.partial(_const_spec, index=layer, ngrid=2)
    weights = [lw['mix_norm_pre'], lw['wcat'], lw['wift'], lw['gbias'], lw['ln_g'], lw['ln_b'],
               lw['mixm_p'], lw['mixb_p'], lw['pool_w'], lw['pool_scale'], lw['norm_g'],
               lw['proj_a'], lw['proj_b'], lw['proj_c'], lw['w_out'], lw['mix_norm_post']]
    out_shapes = (
        jax.ShapeDtypeStruct((nb, seq, D_MODEL), F32),
        jax.ShapeDtypeStruct((nb, POOL_HIST, 512), F32),
        jax.ShapeDtypeStruct((nb, HEADS, DV, DQK), F32),
        jax.ShapeDtypeStruct((nb, 8, DQK), F32),
        jax.ShapeDtypeStruct((nb, 16, BLOCK_ROWS), F32),
    )
    out_specs = (
        pl.BlockSpec((nseq, BLOCK_ROWS, D_MODEL), lambda g, c: (g, c, 0)),
        pl.BlockSpec((nseq, POOL_HIST, 512), lambda g, c: (g, 0, 0)),
        pl.BlockSpec((nseq, HEADS, DV, DQK), lambda g, c: (g, 0, 0, 0)),
        pl.BlockSpec((nseq, 8, DQK), lambda g, c: (g, 0, 0)),
        pl.BlockSpec((nseq, 16, BLOCK_ROWS), lambda g, c: (g, 0, 0)),
    )
    return pl.pallas_call(
        functools.partial(_mix_prompt_kernel, nseq=nseq),
        grid=(nb // nseq, nchunks),
        in_specs=[pl.BlockSpec((nseq, BLOCK_ROWS, D_MODEL), lambda g, c: (g, c, 0)),
                  pl.BlockSpec((nseq, BLOCK_ROWS, D_MODEL), lambda g, c: (g, jnp.minimum(c + 1, nchunks - 1), 0))]
        + [cs(w.shape) for w in weights],
        out_specs=out_specs,
        out_shape=out_shapes,
        scratch_shapes=[pltpu.VMEM((rows, 512), BF16), pltpu.VMEM((rows, 512), BF16),
                        pltpu.VMEM((rows, 512), BF16), pltpu.VMEM((rows, D_MODEL), F32),
                        pltpu.VMEM((rows, 4 * D_MODEL), F32), pltpu.VMEM((2, rows, D_MODEL), BF16)],
        compiler_params=pltpu.CompilerParams(
            dimension_semantics=("parallel", "arbitrary"), vmem_limit_bytes=VMEM_LIMIT_BYTES),
        name="mix_prompt",
    )(x3d, x3d, *weights)


def _mix_s1_kernel(x_ref, hist_ref, gpre_ref, wcat_ref, wifc_ref, lng_ref, lnb_ref, mixm_ref, mixb_ref,
                   poolw_ref, pscale_ref,
                   outa_ref, outb_ref, q_ref, k_ref, v_ref, gi_ref, gf_ref, vn_ref, pool_ref,
                   pooled_scr):
    h, h_pieces = _prenorm(x_ref[...], gpre_ref[...])
    _branch_a(h, h_pieces, wcat_ref, lng_ref, lnb_ref, mixm_ref, mixb_ref, outa_ref, vn_out_ref=vn_ref)

    def hist_out(s, val):
        pool_ref[s] = val

    b_in = _proj(h, wcat_ref, OFF_B, 512)
    q, k, v = _qkv(h, wcat_ref)
    _branch_b(b_in, poolw_ref, pscale_ref, lambda s: hist_ref[s], hist_out, pooled_scr, outb_ref,
              rs=SAMPLE_SEQS_PER_BLOCK, pos1=None)
    for j in range(512 // 128):
        q_ref[j] = q[:, j * 128:(j + 1) * 128]
        k_ref[j] = k[:, j * 128:(j + 1) * 128]
    for j in range(D_MODEL // 128):
        v_ref[j] = v[:, j * 128:(j + 1) * 128]
    gi_ref[...] = _dot(h, wifc_ref[:, 0:HEADS * DQK])
    gf_ref[...] = _dot(h, wifc_ref[:, HEADS * DQK:2 * HEADS * DQK])


def _mix_s2_kernel(q_ref, k_ref, v_ref, gi_ref, gf_ref, ib_ref, fb_ref, normg_ref, c_ref, n_ref, m_ref,
                   cbuf_ref, hc_ref, cn_ref, nn_ref, mn_ref,
                   cq_scr, win_scr, dec_scr):
    del cbuf_ref
    nb = SAMPLE_SEQS_PER_BLOCK
    steps = SAMPLE_STEPS

    vblocks = DV // 128

    def slab(ref, t):
        return ref[t * nb:(t + 1) * nb, :]

    def slab_v(ref, hd, t):
        return jnp.concatenate([ref[hd * vblocks + j, t * nb:(t + 1) * nb, :] for j in range(vblocks)], axis=1)

    def twice(x):
        return jnp.concatenate([x] * vblocks, axis=1)

    m0 = m_ref[...]
    ig = [slab(gi_ref, t) + ib_ref[...] for t in range(steps)]
    lf = [_log_sigmoid(slab(gf_ref, t) + fb_ref[...]) for t in range(steps)]
    b, a, m_t = [], [], []
    run_b = jnp.zeros_like(m0)
    run_max = jnp.full(m0.shape, -jnp.inf, F32)
    for t in range(steps):
        run_b = run_b + lf[t]
        b.append(run_b)
        a.append(ig[t] - run_b)
        run_max = jnp.maximum(run_max, a[t])
        m_t.append(run_b + jnp.maximum(m0, run_max))
    inter = [jnp.exp(b[t] + m0 - m_t[t]) for t in range(steps)]
    m_end = m_t[-1]
    decay = jnp.exp(b[-1] + m0 - m_end)
    w_in = [jnp.exp(a[t] + b[-1] - m_end) for t in range(steps)]
    mn_ref[...] = m_end
    for hd in range(HEADS):
        hcols = slice(hd * DQK, (hd + 1) * DQK)
        for t in range(steps):
            win_scr[hd, t * nb:(t + 1) * nb, :] = w_in[t][:, hcols]
            dec_scr[hd, t * nb:(t + 1) * nb, :] = decay[:, hcols]

    def seq_body(bi, carry):
        rows = pl.ds(bi, steps, stride=nb)
        for hd in range(HEADS):
            c_old = c_ref[bi, hd]
            qb = q_ref[hd, rows, :].astype(BF16)
            cq = _dot_nt(qb, c_old.astype(BF16))
            for j in range(vblocks):
                cq_scr[hd * vblocks + j, rows, :] = cq[:, j * 128:(j + 1) * 128]
            vrows = jnp.concatenate([v_ref[hd * vblocks + j, rows, :] for j in range(vblocks)], axis=1)
            vs = (vrows * twice(win_scr[hd, rows, :])).astype(BF16)
            kb = k_ref[hd, rows, :].astype(BF16)
            dec_row = dec_scr[hd, rows, :][0:1, :]
            cn_ref[bi, hd] = dec_row * c_old + _dot_tn(vs, kb)
        return carry

    lax.fori_loop(0, nb, seq_body, 0, unroll=SAMPLE_SEQ_UNROLL)

    pairs = [(t, s) for t in range(steps) for s in range(t + 1)]
    dfac = [jnp.exp((b[t] - m_t[t]) + a[s]) for (t, s) in pairs]
    enm = [jnp.exp(-m_t[t]) for t in range(steps)]
    for hd in range(HEADS):
        hcols = slice(hd * DQK, (hd + 1) * DQK)
        vcols = slice(hd * DV, (hd + 1) * DV)
        qs = [slab(q_ref.at[hd], t) for t in range(steps)]
        ks = [slab(k_ref.at[hd], t) for t in range(steps)]
        vs = [slab_v(v_ref, hd, t) for t in range(steps)]
        n_old = n_ref[:, hcols]
        n_new = decay[:, hcols] * n_old
        for t in range(steps):
            n_new = n_new + w_in[t][:, hcols] * ks[t]
        nn_ref[:, hcols] = n_new
        prods = jnp.concatenate([qs[t] * ks[s] for (t, s) in pairs] + [qs[t] * n_old for t in range(steps)], axis=0)
        red = jnp.sum(prods, axis=1, keepdims=True)
        wts = red[:len(pairs) * nb] * jnp.concatenate([d[:, hcols] for d in dfac], axis=0)
        for t in range(steps):
            inter_t = inter[t][:, hcols]
            num = twice(inter_t) * slab_v(cq_scr, hd, t)
            den = inter_t * red[(len(pairs) + t) * nb:(len(pairs) + t + 1) * nb]
            for s in range(t + 1):
                p = pairs.index((t, s))
                w_ts = wts[p * nb:(p + 1) * nb]
                num = num + twice(w_ts) * vs[s]
                den = den + w_ts
            hout = num / twice(jnp.maximum(jnp.abs(den), enm[t][:, hcols]))
            hc_ref[t * nb:(t + 1) * nb, vcols] = _rms(hout, normg_ref[:, vcols])


def _mix_s3_kernel(x_ref, outa_ref, outb_ref, hc_ref, gpre_ref, wgate_ref, pa_ref, pb_ref, pc_ref,
                   wout_ref, gpost_ref, o_ref):
    x = x_ref[...]
    h, h_pieces = _prenorm(x, gpre_ref[...])

    def gate(j):
        pieces = h_pieces if j == 0 else [h]
        return _sigmoid(_rows_dot_nt(pieces, _gate_weight(j, D_MODEL, wgate_ref.at[0], row0=OFF_O)))

    o_ref[...] = _merge(x, gate, hc_ref[...], pa_ref, pb_ref, pc_ref, wout_ref, gpost_ref, outa_ref, outb_ref)


def _mix_sample(xs, hist, c_all, c_buf, n_in, m_in, lw, layer):
    rows = xs.shape[0]
    nblk = rows // BLOCK_ROWS
    hist_rows = POOL_HIST * SAMPLE_SEQS_PER_BLOCK
    cs1 = functools.partial(_const_spec, index=layer, ngrid=1)
    full = lambda shape: pl.BlockSpec(shape, lambda i: (0,) * len(shape))
    params = pltpu.CompilerParams(dimension_semantics=("arbitrary",), vmem_limit_bytes=VMEM_LIMIT_BYTES)

    w1 = [lw['mix_norm_pre'], lw['wcat'], lw['wifc'], lw['ln_g'], lw['ln_b'], lw['mixm_s'], lw['mixb_s'],
          lw['pool_w'], lw['pool_scale']]
    s1_specs = [cs1(w.shape) for w in w1]
    s1_specs[1] = pl.BlockSpec((None, OFF_O, D_MODEL), lambda i: (layer, 0, 0), pipeline_mode=pl.Buffered(1))
    s1_shapes = (
        jax.ShapeDtypeStruct((rows, 512), BF16),
        jax.ShapeDtypeStruct((rows, 512), BF16),
        jax.ShapeDtypeStruct((HEADS, rows, DQK), F32),
        jax.ShapeDtypeStruct((HEADS, rows, DQK), F32),
        jax.ShapeDtypeStruct((D_MODEL // 128, rows, 128), F32),
        jax.ShapeDtypeStruct((rows, HEADS * DQK), F32),
        jax.ShapeDtypeStruct((rows, HEADS * DQK), F32),
        jax.ShapeDtypeStruct((rows, 512), F32),
        jax.ShapeDtypeStruct((nblk, hist_rows, 512), F32),
    )
    outa, outb, q, k, v, gi, gf, vn, pool_new = pl.pallas_call(
        _mix_s1_kernel,
        grid=(1,),
        in_specs=[full(xs.shape), full(hist.shape)] + s1_specs,
        out_specs=tuple(full(s.shape) for s in s1_shapes),
        out_shape=s1_shapes,
        scratch_shapes=[pltpu.VMEM((rows, 512), BF16)],
        compiler_params=params,
        name="mix_s1",
    )(xs, hist, *w1)

    nb = SAMPLE_SEQS_PER_BLOCK
    blk = lambda width: pl.BlockSpec((BLOCK_ROWS, width), lambda j: (j, 0))
    seqblk = lambda width: pl.BlockSpec((nb, width), lambda j: (j, 0))
    cblk = pl.BlockSpec((None, nb, HEADS, DV, DQK), lambda j: (layer, j, 0, 0, 0))
    lrow = lambda width: pl.BlockSpec((None, 1, width), lambda j: (layer, 0, 0))
    colblk = lambda n: pl.BlockSpec((n, BLOCK_ROWS, 128), lambda j: (0, j, 0))
    hc, c_buf, n_new, m_new = pl.pallas_call(
        _mix_s2_kernel,
        grid=(nblk,),
        in_specs=[colblk(HEADS), colblk(HEADS), colblk(D_MODEL // 128), blk(512), blk(512),
                  lrow(512), lrow(512), lrow(D_MODEL), cblk, seqblk(512), seqblk(512),
                  pl.BlockSpec(memory_space=pl.ANY)],
        out_specs=(blk(D_MODEL), cblk, seqblk(512), seqblk(512)),
        out_shape=(jax.ShapeDtypeStruct((rows, D_MODEL), F32),
                   jax.ShapeDtypeStruct(c_all.shape, F32),
                   jax.ShapeDtypeStruct(n_in.shape, F32),
                   jax.ShapeDtypeStruct(m_in.shape, F32)),
        input_output_aliases={11: 1},
        scratch_shapes=[pltpu.VMEM((D_MODEL // 128, BLOCK_ROWS, 128), F32),
                        pltpu.VMEM((HEADS, BLOCK_ROWS, 128), F32), pltpu.VMEM((HEADS, BLOCK_ROWS, 128), F32)],
        compiler_params=pltpu.CompilerParams(dimension_semantics=("parallel",),
                                             vmem_limit_bytes=VMEM_LIMIT_BYTES),
        name="mix_s2",
    )(q, k, v, gi, gf, lw['ib'], lw['fb'], lw['norm_g'], c_all, n_in, m_in, c_buf)

    w3 = [lw['mix_norm_pre'], lw['wcat'], lw['proj_a'], lw['proj_b'], lw['proj_c'], lw['w_out'],
          lw['mix_norm_post']]
    s3_specs = [cs1(w.shape) for w in w3]
    s3_specs[1] = pl.BlockSpec((pl.Element(1), pl.Element(WT_ROWS - OFF_O), pl.Element(D_MODEL)),
                               lambda i: (layer, OFF_O, 0), pipeline_mode=pl.Buffered(1))
    y = pl.pallas_call(
        _mix_s3_kernel,
        grid=(1,),
        in_specs=[full(xs.shape), full(outa.shape), full(outb.shape), full(hc.shape)] + s3_specs,
        out_specs=full(xs.shape),
        out_shape=jax.ShapeDtypeStruct(xs.shape, F32),
        compiler_params=params,
        name="mix_s3",
    )(xs, outa, outb, hc, *w3)
    return y, vn, pool_new, c_buf, n_new, m_new


def _to_blocked(x):
    nseq, t, c = x.shape
    nb = SAMPLE_SEQS_PER_BLOCK
    return x.reshape(nseq // nb, nb, t, c).transpose(0, 2, 1, 3).reshape(nseq * t, c)


def _from_blocked(x, t):
    rows, c = x.shape
    nb = SAMPLE_SEQS_PER_BLOCK
    nseq = rows // t
    return x.reshape(nseq // nb, t, nb, c).transpose(0, 2, 1, 3).reshape(nseq, t, c)


def _cast_rows_kernel(w_ref, o_ref):
    o_ref[...] = w_ref[0].astype(BF16)


def _input_proj_weights(w_in):
    w_t = jnp.swapaxes(w_in, 1, 2)
    rows = 512
    src_row = lambda l, r: (l, pl.multiple_of(r * rows + jnp.where(r * rows >= OFF_G, GATE_COLS, 0), 8), 0)
    return pl.pallas_call(
        _cast_rows_kernel,
        grid=(DEPTH, WT_ROWS // rows),
        in_specs=[pl.BlockSpec((pl.Element(1), pl.Element(rows), pl.Element(D_MODEL)), src_row)],
        out_specs=pl.BlockSpec((None, rows, D_MODEL), lambda l, r: (l, r, 0)),
        out_shape=jax.ShapeDtypeStruct((DEPTH, WT_ROWS, D_MODEL), BF16),
        compiler_params=pltpu.CompilerParams(
            dimension_semantics=("parallel", "parallel"), vmem_limit_bytes=VMEM_LIMIT_BYTES),
        name="cast_w_in",
    )(w_t)


def _prep_weights(p):
    w_in = p['w_in']
    row = lambda a: a.reshape(DEPTH, 1, -1).astype(F32)
    wi = w_in[:, :, OFF_G:OFF_G + HEADS]
    wf = w_in[:, :, OFF_G + HEADS:OFF_G + 2 * HEADS]
    wift = jnp.zeros((DEPTH, 32, D_MODEL), F32)
    wift = wift.at[:, 0:HEADS].set(wi.transpose(0, 2, 1)).at[:, 16:16 + HEADS].set(wf.transpose(0, 2, 1))
    wifc = jnp.concatenate([jnp.repeat(wi, DQK, axis=2), jnp.repeat(wf, DQK, axis=2)], axis=2)
    gbias = jnp.zeros((DEPTH, 32, BLOCK_ROWS), F32)
    gbias = gbias.at[:, 0:HEADS].set(jnp.broadcast_to(p['mlstm_i_bias'][:, :, None], (DEPTH, HEADS, BLOCK_ROWS)))
    gbias = gbias.at[:, 16:16 + HEADS].set(
        jnp.broadcast_to(p['mlstm_f_bias'][:, :, None], (DEPTH, HEADS, BLOCK_ROWS)))
    lane_pad = lambda a: jnp.repeat(a.astype(F32), DQK, axis=1).reshape(DEPTH, 1, HEADS * DQK)

    ws = p['gmlp_w_s']
    bs = p['gmlp_b_s']
    mixm_p = jnp.tril(ws)
    mixb_p = jnp.broadcast_to(bs[..., None], ws.shape)
    nb = SAMPLE_SEQS_PER_BLOCK
    tril_s = jnp.tril(ws[:, :, :SAMPLE_STEPS, :SAMPLE_STEPS])
    eye = jnp.eye(nb, dtype=F32)
    mixm_s = jnp.einsum('lgts,bc->lgtbsc', tril_s, eye).reshape(DEPTH, N_GROUPS, BLOCK_ROWS, BLOCK_ROWS)
    mixb_s = jnp.broadcast_to(jnp.repeat(bs[:, :, :SAMPLE_STEPS], nb, axis=2)[..., None], ws.shape)

    return dict(
        ffn1_norm_pre=row(p['ffn1_norm_pre']), ffn1_w_in=p['ffn1_w_in'].astype(BF16),
        ffn1_w_down=p['ffn1_w_down'].astype(BF16), ffn1_norm_post=row(p['ffn1_norm_post']),
        ffn2_norm_pre=row(p['ffn2_norm_pre']), ffn2_w_in=p['ffn2_w_in'].astype(BF16),
        ffn2_w_down=p['ffn2_w_down'].astype(BF16), ffn2_norm_post=row(p['ffn2_norm_post']),
        mix_norm_pre=row(p['mix_norm_pre']), mix_norm_post=row(p['mix_norm_post']),
        wcat=_input_proj_weights(w_in),
        wift=wift.astype(BF16), wifc=wifc.astype(BF16), gbias=gbias,
        ib=lane_pad(p['mlstm_i_bias']), fb=lane_pad(p['mlstm_f_bias']),
        ln_g=row(p['gmlp_ln_g']), ln_b=row(p['gmlp_ln_b']),
        mixm_p=mixm_p.astype(BF16), mixb_p=mixb_p.astype(F32),
        mixm_s=mixm_s.astype(BF16), mixb_s=mixb_s.astype(F32),
        pool_w=p['pool_w'].astype(BF16), pool_scale=row(p['pool_scale']), norm_g=row(p['mlstm_norm_g']),
        proj_a=p['proj_a'].astype(BF16), proj_b=p['proj_b'].astype(BF16), proj_c=p['proj_c'].astype(BF16),
        w_out=p['w_out'].astype(BF16),
    )


def kernel(x_prompt, x_sample, state_pool, state_mlstm_C, state_mlstm_n, state_mlstm_m, ffn1_norm_pre, ffn1_w_in, ffn1_w_down, ffn1_norm_post, mix_norm_pre, w_in, gmlp_ln_g, gmlp_ln_b, gmlp_w_s, gmlp_b_s, pool_w, pool_scale, mlstm_i_bias, mlstm_f_bias, mlstm_norm_g, proj_a, proj_b, proj_c, w_out, mix_norm_post, ffn2_norm_pre, ffn2_w_in, ffn2_w_down, ffn2_norm_post):
    params = dict(ffn1_norm_pre=ffn1_norm_pre, ffn1_w_in=ffn1_w_in, ffn1_w_down=ffn1_w_down,
                  ffn1_norm_post=ffn1_norm_post, mix_norm_pre=mix_norm_pre, w_in=w_in, gmlp_ln_g=gmlp_ln_g,
                  gmlp_ln_b=gmlp_ln_b, gmlp_w_s=gmlp_w_s, gmlp_b_s=gmlp_b_s, pool_w=pool_w,
                  pool_scale=pool_scale, mlstm_i_bias=mlstm_i_bias, mlstm_f_bias=mlstm_f_bias,
                  mlstm_norm_g=mlstm_norm_g, proj_a=proj_a, proj_b=proj_b, proj_c=proj_c, w_out=w_out,
                  mix_norm_post=mix_norm_post, ffn2_norm_pre=ffn2_norm_pre, ffn2_w_in=ffn2_w_in,
                  ffn2_w_down=ffn2_w_down, ffn2_norm_post=ffn2_norm_post)
    lw = _prep_weights(params)
    nbp, seq, _ = x_prompt.shape
    nbs, steps, _ = x_sample.shape

    yp = x_prompt
    ys = _to_blocked(x_sample)
    hist_all = jnp.pad(state_pool, ((0, 0), (0, 0), (1, 0), (0, 0)))
    n_all = state_mlstm_n.reshape(DEPTH, nbs, HEADS * DQK)
    m_all = jnp.repeat(state_mlstm_m, DQK, axis=2)

    pool_p, c_p, n_p, m_p = [], [], [], []
    pool_s, n_s, m_s, v_s = [], [], [], []
    c_sample = jnp.zeros(state_mlstm_C.shape, F32)
    for l in range(DEPTH):
        ffn1 = (lw['ffn1_norm_pre'], lw['ffn1_w_in'], lw['ffn1_w_down'], lw['ffn1_norm_post'])
        ffn2 = (lw['ffn2_norm_pre'], lw['ffn2_w_in'], lw['ffn2_w_down'], lw['ffn2_norm_post'])
        yp = _ffn(yp.reshape(nbp * seq, D_MODEL), *ffn1, l).reshape(nbp, seq, D_MODEL)
        yp, pb, cf, nf_, mf = _mix_prompt(yp, lw, l)
        yp = _ffn(yp.reshape(nbp * seq, D_MODEL), *ffn2, l).reshape(nbp, seq, D_MODEL)
        pool_p.append(pb[:, 1:, :])
        c_p.append(cf)
        n_p.append(nf_[:, :HEADS, :])
        m_p.append(mf[:, :HEADS, 0])
        ys = _ffn(ys, *ffn1, l)
        hist = _to_blocked(hist_all[l]).reshape(nbs // SAMPLE_SEQS_PER_BLOCK,
                                                POOL_HIST * SAMPLE_SEQS_PER_BLOCK, 512)
        ys, vn, pool_new, c_sample, n_new, m_new = _mix_sample(
            ys, hist, state_mlstm_C, c_sample, n_all[l], m_all[l], lw, l)
        ys = _ffn(ys, *ffn2, l)
        pool_s.append(_from_blocked(pool_new.reshape(nbs * POOL_HIST, 512), POOL_HIST)[:, 1:, :])
        n_s.append(n_new.reshape(nbs, HEADS, DQK))
        m_s.append(m_new.reshape(nbs, HEADS, DQK)[:, :, 0])
        v_s.append(_from_blocked(vn, steps))
    return (yp, _from_blocked(ys, steps),
            jnp.stack(pool_p), jnp.stack(c_p), jnp.stack(n_p), jnp.stack(m_p),
            jnp.stack(pool_s), c_sample, jnp.stack(n_s), jnp.stack(m_s), jnp.stack(v_s))
```

```python
import functools

import jax
import jax.numpy as jnp
from jax import lax
from jax.experimental import pallas as pl
from jax.experimental.pallas import tpu as pltpu

F32 = jnp.float32
BF16 = jnp.bfloat16

D_MODEL = 1024
D_FF = 2816
DEPTH = 4
EPS = 1e-6
PAST_LEN = 16384
N_GROUPS = 4
GROUP_W = 128
POOL_WINDOWS = (2, 4, 8, 16)
POOL_HIST = 16
HEADS = 4
DQK = 128
DV = 256
BLOCK_ROWS = 128
SAMPLE_SEQS_PER_BLOCK = 16
SAMPLE_STEPS = 8

OFF_AU, OFF_AV, OFF_B, OFF_Q, OFF_K, OFF_V, OFF_O = 0, 512, 1024, 1536, 2048, 2560, 3584
GATE_COLS = 8
OFF_G = 4608
WT_ROWS = OFF_G + 3 * D_MODEL

SAMPLE_SEQ_UNROLL = 4

FFN_TM = 2048
FFN_TF = 256
MIX_ROW_SPLIT = 2
FFN_ROW_BLOCKS = 4
PROMPT_SEQS_PER_STEP = 4
VMEM_LIMIT_BYTES = 56 * 1024 * 1024


def _dot(a, b):
    return jnp.dot(a, b, preferred_element_type=F32)


def _dot_nt(a, b):
    return lax.dot_general(a, b, (((1,), (1,)), ((), ())), preferred_element_type=F32)


def _dot_tn(a, b):
    return lax.dot_general(a, b, (((0,), (0,)), ((), ())), preferred_element_type=F32)


def _rms(x, g):
    return x * lax.rsqrt(jnp.mean(x * x, axis=-1, keepdims=True) + EPS) * g


def _sigmoid(x):
    return 1.0 / (1.0 + jnp.exp(-x))


def _gelu(x):
    return 0.5 * x * (1.0 + jnp.tanh(0.7978845608028654 * (x + 0.044715 * (x * x * x))))


def _log_sigmoid(x):
    return jnp.minimum(x, 0.0) - jnp.log1p(jnp.exp(-jnp.abs(x)))


def _const_spec(shape, index, ngrid):
    zeros = (0,) * (len(shape) - 1)
    if ngrid == 1:
        imap = lambda i: (index,) + zeros
    else:
        imap = lambda i, j: (index,) + zeros
    return pl.BlockSpec((None,) + tuple(shape[1:]), imap, pipeline_mode=pl.Buffered(1))


def _ffn_tiles(m):
    if m > FFN_TM:
        return FFN_TM, FFN_TF, FFN_ROW_BLOCKS
    return m, D_FF // 2, FFN_ROW_BLOCKS // 2


def _ffn_kernel(x_ref, gpre_ref, wgate_ref, wup_ref, wdown_ref, gpost_ref, o_ref, h_scr, *, row_blocks, nf):
    f = pl.program_id(1)
    last = nf - 1
    tm = x_ref.shape[0]
    rb = tm // row_blocks

    def step(first, final):
        wgate = wgate_ref[...]
        wup = wup_ref[...]
        wdown = wdown_ref[...]
        for r in range(row_blocks):
            rows = slice(r * rb, (r + 1) * rb)
            if first:
                h = _rms(x_ref[rows, :], gpre_ref[...]).astype(BF16)
                h_scr[rows, :] = h
            else:
                h = h_scr[rows, :]
            gate = _dot(h, wgate)
            up = _dot(h, wup)
            act = (gate * _sigmoid(gate) * up).astype(BF16)
            part = _dot(act, wdown)
            acc = part if first else o_ref[rows, :] + part
            if final:
                acc = x_ref[rows, :] + _rms(acc, 0.5 * gpost_ref[...])
            o_ref[rows, :] = acc

    @pl.when(f == 0)
    def _():
        step(True, False)

    if nf > 2:
        @pl.when(jnp.logical_and(f > 0, f < last))
        def _():
            step(False, False)

    @pl.when(f == last)
    def _():
        step(False, True)


def _ffn(x2d, gpre, w_in, wdown, gpost, layer):
    m = x2d.shape[0]
    tm, tf, row_blocks = _ffn_tiles(m)
    nf = D_FF // tf
    return pl.pallas_call(
        functools.partial(_ffn_kernel, row_blocks=row_blocks, nf=nf),
        grid=(m // tm, nf),
        in_specs=[
            pl.BlockSpec((tm, D_MODEL), lambda i, f: (i, 0)),
            pl.BlockSpec((None, 1, D_MODEL), lambda i, f: (layer, 0, 0)),
            pl.BlockSpec((None, D_MODEL, tf), lambda i, f: (layer, 0, f)),
            pl.BlockSpec((None, D_MODEL, tf), lambda i, f: (layer, 0, f + nf)),
            pl.BlockSpec((None, tf, D_MODEL), lambda i, f: (layer, f, 0)),
            pl.BlockSpec((None, 1, D_MODEL), lambda i, f: (layer, 0, 0)),
        ],
        out_specs=pl.BlockSpec((tm, D_MODEL), lambda i, f: (i, 0)),
        out_shape=jax.ShapeDtypeStruct((m, D_MODEL), F32),
        scratch_shapes=[pltpu.VMEM((tm, D_MODEL), BF16)],
        compiler_params=pltpu.CompilerParams(
            dimension_semantics=("parallel", "arbitrary"), vmem_limit_bytes=VMEM_LIMIT_BYTES),
        name="ffn",
    )(x2d, gpre, w_in, w_in, wdown, gpost)


def _prenorm(x, g):
    step = x.shape[0] // MIX_ROW_SPLIT
    pieces = [_rms(x[i * step:(i + 1) * step, :], g).astype(BF16) for i in range(MIX_ROW_SPLIT)]
    return jnp.concatenate(pieces, axis=0), pieces


def _proj(h, w_ref, lo, n):
    return _dot(h, w_ref[:, lo:lo + n])


def _rows_dot(pieces, w):
    return jnp.concatenate([_dot(p, w) for p in pieces], axis=0)


def _branch_a(h, h_pieces, wcat_ref, lng_ref, lnb_ref, mixm_ref, mixb_ref, outa_scr, vn_out_ref=None):
    rows = h.shape[0]
    a_v = _gelu(_rows_dot(h_pieces, wcat_ref[:, OFF_AV:OFF_AV + 512]))
    mu = jnp.mean(a_v, axis=-1, keepdims=True)
    xc = a_v - mu
    var = jnp.mean(xc * xc, axis=-1, keepdims=True)
    v_n = xc * lax.rsqrt(var + EPS) * lng_ref[...] + lnb_ref[...]
    if vn_out_ref is not None:
        vn_out_ref[...] = v_n
    vb = v_n.astype(BF16)
    a_u = _gelu(_proj(h, wcat_ref, OFF_AU, 512))
    for g in range(N_GROUPS):
        cols = slice(g * GROUP_W, (g + 1) * GROUP_W)
        mix_g = mixm_ref[g]
        bias_g = mixb_ref[g]
        for s in range(rows // BLOCK_ROWS):
            rws = slice(s * BLOCK_ROWS, (s + 1) * BLOCK_ROWS)
            mixed = _dot(mix_g, vb[rws, cols]) + bias_g
            outa_scr[rws, cols] = (a_u[rws, cols] * mixed).astype(BF16)


def _branch_b(b_in, poolw_ref, pscale_ref, hist_in, hist_out, pooled_scr, outb_scr, *, rs, pos1):
    rows = b_in.shape[0]
    hist_rows = POOL_HIST * rs
    for s in range(rows // BLOCK_ROWS):
        rws = slice(s * BLOCK_ROWS, (s + 1) * BLOCK_ROWS)
        cur = b_in[rws, :]
        ext = jnp.concatenate([hist_in(s), cur], axis=0)
        for g, w in enumerate(POOL_WINDOWS):
            cols = slice(g * GROUP_W, (g + 1) * GROUP_W)
            acc = ext[:, cols]
            sh = 1
            while sh < w:
                acc = acc + pltpu.roll(acc, sh * rs, axis=0)
                sh *= 2
            win = acc[hist_rows:, :]
            if pos1 is None:
                mean = win * (1.0 / w)
            else:
                mean = win / jnp.minimum(pos1, w).astype(F32)
            pooled_scr[rws, cols] = (mean - cur[:, cols]).astype(BF16)
        hist_out(s, ext[BLOCK_ROWS:, :])
    for g in range(N_GROUPS):
        cols = slice(g * GROUP_W, (g + 1) * GROUP_W)
        mixed = _dot(pooled_scr[:, cols], poolw_ref[g]) * pscale_ref[:, cols]
        outb_scr[:, cols] = mixed.astype(BF16)


def _gate_weight(j, width, wcat_ref, col0=0):
    lo = j * width
    lo = OFF_G + lo if lo < 3 * D_MODEL else OFF_O + lo - 3 * D_MODEL
    return wcat_ref[:, lo - col0:lo - col0 + width]


def _merge(x, gate, hc, pa_ref, pb_ref, pc_ref, wout_ref, gpost_ref, outa_scr, outb_scr):
    merged = gate(0) * _dot(outa_scr[...], pa_ref[...])
    merged += gate(1) * _dot(outb_scr[...], pb_ref[...])
    out_c = (gate(3) * hc).astype(BF16)
    merged += gate(2) * _dot(out_c, pc_ref[...])
    mb = merged.astype(BF16)
    step = x.shape[0] // MIX_ROW_SPLIT
    outs = []
    for i in range(MIX_ROW_SPLIT):
        rws = slice(i * step, (i + 1) * step)
        outs.append(x[rws, :] + _rms(_dot(mb[rws, :], wout_ref[...]), gpost_ref[...]))
    return jnp.concatenate(outs, axis=0)


def _scan_lanes(x, op, fill):
    lane = lax.broadcasted_iota(jnp.int32, x.shape, 1)
    sh = 1
    while sh < x.shape[1]:
        x = op(x, jnp.where(lane >= sh, pltpu.roll(x, sh, axis=1), fill))
        sh *= 2
    return x


def _last_lane(x):
    lane = lax.broadcasted_iota(jnp.int32, x.shape, 1)
    return jnp.max(jnp.where(lane == x.shape[1] - 1, x, -jnp.inf), axis=1, keepdims=True)


def _mlstm_gates(h_pieces, wift_ref, gbias_ref, m_ref, *, nseq):
    t_len = BLOCK_ROWS
    grow = jnp.concatenate([_dot_nt(wift_ref[...], p) for p in h_pieces], axis=1)
    seqs = range(nseq)
    ig = jnp.concatenate([grow[0:16, s * t_len:(s + 1) * t_len] + gbias_ref[0:16, :] for s in seqs], axis=0)
    lf = _log_sigmoid(
        jnp.concatenate([grow[16:32, s * t_len:(s + 1) * t_len] + gbias_ref[16:32, :] for s in seqs], axis=0))
    m0 = m_ref[...].reshape(nseq * 16, t_len)
    b = _scan_lanes(lf, jnp.add, 0.0)
    a = ig - b
    m_t = b + jnp.maximum(m0, _scan_lanes(a, jnp.maximum, -jnp.inf))
    inter = jnp.exp(b + m0 - m_t)
    b_last = _last_lane(b)
    m_end = _last_lane(m_t)
    decay = jnp.exp(b_last + m0 - m_end)
    w_in = jnp.exp(a + b_last - m_end)
    m_ref[...] = jnp.broadcast_to(m_end, (nseq * 16, t_len)).reshape(nseq, 16, t_len)
    stacked = jnp.concatenate([b - m_t, inter, jnp.exp(-m_t), w_in], axis=0)
    pad = (-stacked.shape[0]) % t_len
    if pad:
        stacked = jnp.concatenate([stacked, jnp.zeros((pad, t_len), F32)], axis=0)
    cols_t = [stacked[i * t_len:(i + 1) * t_len, :].T for i in range(stacked.shape[0] // t_len)]

    def col(quantity, s, hd):
        idx = quantity * nseq * 16 + s * 16 + hd
        return cols_t[idx // t_len][:, idx % t_len:idx % t_len + 1]

    return a, decay, col


def _qkv(h, wcat_ref):
    q = _proj(h, wcat_ref, OFF_Q, 512)
    k = _proj(h, wcat_ref, OFF_K, 512) * (DQK ** -0.5)
    v = _proj(h, wcat_ref, OFF_V, D_MODEL)
    return q, k, v


def _mlstm_prompt(qkv, gates, normg_ref, c_ref, n_ref, hc_scr, *, nseq, fillers):
    t_len = BLOCK_ROWS
    a_all, decay_all, col = gates
    q, k, v = qkv
    ri = lax.broadcasted_iota(jnp.int32, (t_len, t_len), 0)
    ci = lax.broadcasted_iota(jnp.int32, (t_len, t_len), 1)
    causal = ci <= ri
    for s in range(nseq):
        rws = slice(s * t_len, (s + 1) * t_len)
        a = a_all[s * 16:(s + 1) * 16, :]
        decay = decay_all[s * 16:(s + 1) * 16, :]
        for hd in range(HEADS):
            fillers[s * HEADS + hd]()
            qh = q[rws, hd * DQK:(hd + 1) * DQK]
            kh = k[rws, hd * DQK:(hd + 1) * DQK]
            vh = v[rws, hd * DV:(hd + 1) * DV]
            qb = qh.astype(BF16)
            kb = kh.astype(BF16)
            bm_c = col(0, s, hd)
            inter_c = col(1, s, hd)
            en_c = col(2, s, hd)
            win_c = col(3, s, hd)
            dmat = jnp.exp(jnp.where(causal, bm_c + a[hd:hd + 1, :], -jnp.inf))
            wmat = _dot_nt(qb, kb) * dmat
            c_old = c_ref[s, hd]
            n_old = n_ref[s, hd:hd + 1, :]
            num = _dot(wmat.astype(BF16), vh.astype(BF16)) + inter_c * _dot_nt(qb, c_old.astype(BF16))
            den = jnp.sum(wmat, axis=1, keepdims=True) + inter_c * jnp.sum(qh * n_old, axis=1, keepdims=True)
            hout = num / jnp.maximum(jnp.abs(den), en_c)
            hc_scr[rws, hd * DV:(hd + 1) * DV] = _rms(hout, normg_ref[:, hd * DV:(hd + 1) * DV])
            dec_row = decay[hd:hd + 1, :]
            c_ref[s, hd] = dec_row * c_old + _dot_tn((vh * win_c).astype(BF16), kb)
            n_ref[s, hd:hd + 1, :] = dec_row * n_old + jnp.sum(kh * win_c, axis=0, keepdims=True)


def _mix_prompt_kernel(x_ref, gpre_ref, wcat_ref, wift_ref, gbias_ref, lng_ref, lnb_ref,
                       mixm_ref, mixb_ref, poolw_ref, pscale_ref, normg_ref, pa_ref, pb_ref, pc_ref, wout_ref,
                       gpost_ref,
                       o_ref, pool_ref, c_ref, n_ref, m_ref,
                       outa_scr, outb_scr, pooled_scr, hc_scr, gate_scr, *, nseq):
    c = pl.program_id(1)
    rows = nseq * BLOCK_ROWS

    @pl.when(c == 0)
    def _():
        pool_ref[...] = jnp.zeros(pool_ref.shape, F32)
        c_ref[...] = jnp.zeros(c_ref.shape, F32)
        n_ref[...] = jnp.zeros(n_ref.shape, F32)
        m_ref[...] = jnp.zeros(m_ref.shape, F32)

    x = x_ref[...].reshape(rows, D_MODEL)
    h, h_pieces = _prenorm(x, gpre_ref[...])

    gates = _mlstm_gates(h_pieces, wift_ref, gbias_ref, m_ref, nseq=nseq)

    _branch_a(h, h_pieces, wcat_ref, lng_ref, lnb_ref, mixm_ref, mixb_ref, outa_scr)

    b_in = _proj(h, wcat_ref, OFF_B, 512)
    qkv = _qkv(h, wcat_ref)

    pos1 = c * BLOCK_ROWS + 1 + lax.broadcasted_iota(jnp.int32, (BLOCK_ROWS, GROUP_W), 0)

    def hist_out(s, val):
        pool_ref[s] = val

    _branch_b(b_in, poolw_ref, pscale_ref, lambda s: pool_ref[s], hist_out, pooled_scr, outb_scr,
              rs=1, pos1=pos1)

    n_fill = nseq * HEADS
    fill_w = 4 * D_MODEL // n_fill

    def make_filler(j):
        def fill():
            gate_scr[:, j * fill_w:(j + 1) * fill_w] = _sigmoid(_dot(h, _gate_weight(j, fill_w, wcat_ref)))
        return fill

    _mlstm_prompt(qkv, gates, normg_ref, c_ref, n_ref, hc_scr, nseq=nseq,
                  fillers=[make_filler(j) for j in range(n_fill)])

    y = _merge(x, lambda j: gate_scr[:, j * D_MODEL:(j + 1) * D_MODEL], hc_scr[...],
               pa_ref, pb_ref, pc_ref, wout_ref, gpost_ref, outa_scr, outb_scr)
    o_ref[...] = y.reshape(nseq, BLOCK_ROWS, D_MODEL)


def _mix_prompt(x3d, lw, layer):
    nb, seq, _ = x3d.shape
    nseq = PROMPT_SEQS_PER_STEP
    rows = nseq * BLOCK_ROWS
    nchunks = seq // BLOCK_ROWS
    cs = functools.partial(_const_spec, index=layer, ngrid=2)
    weights = [lw['mix_norm_pre'], lw['wcat'], lw['wift'], lw['gbias'], lw['ln_g'], lw['ln_b'],
               lw['mixm_p'], lw['mixb_p'], lw['pool_w'], lw['pool_scale'], lw['norm_g'],
               lw['proj_a'], lw['proj_b'], lw['proj_c'], lw['w_out'], lw['mix_norm_post']]
    out_shapes = (
        jax.ShapeDtypeStruct((nb, seq, D_MODEL), F32),
        jax.ShapeDtypeStruct((nb, POOL_HIST, 512), F32),
        jax.ShapeDtypeStruct((nb, HEADS, DV, DQK), F32),
        jax.ShapeDtypeStruct((nb, 8, DQK), F32),
        jax.ShapeDtypeStruct((nb, 16, BLOCK_ROWS), F32),
    )
    out_specs = (
        pl.BlockSpec((nseq, BLOCK_ROWS, D_MODEL), lambda g, c: (g, c, 0)),
        pl.BlockSpec((nseq, POOL_HIST, 512), lambda g, c: (g, 0, 0)),
        pl.BlockSpec((nseq, HEADS, DV, DQK), lambda g, c: (g, 0, 0, 0)),
        pl.BlockSpec((nseq, 8, DQK), lambda g, c: (g, 0, 0)),
        pl.BlockSpec((nseq, 16, BLOCK_ROWS), lambda g, c: (g, 0, 0)),
    )
    return pl.pallas_call(
        functools.partial(_mix_prompt_kernel, nseq=nseq),
        grid=(nb // nseq, nchunks),
        in_specs=[pl.BlockSpec((nseq, BLOCK_ROWS, D_MODEL), lambda g, c: (g, c, 0))]
        + [cs(w.shape) for w in weights],
        out_specs=out_specs,
        out_shape=out_shapes,
        scratch_shapes=[pltpu.VMEM((rows, 512), BF16), pltpu.VMEM((rows, 512), BF16),
                        pltpu.VMEM((rows, 512), BF16), pltpu.VMEM((rows, D_MODEL), F32),
                        pltpu.VMEM((rows, 4 * D_MODEL), F32)],
        compiler_params=pltpu.CompilerParams(
            dimension_semantics=("parallel", "arbitrary"), vmem_limit_bytes=VMEM_LIMIT_BYTES),
        name="mix_prompt",
    )(x3d, *weights)


def _mix_s1_kernel(x_ref, hist_ref, gpre_ref, wcat_ref, wifc_ref, lng_ref, lnb_ref, mixm_ref, mixb_ref,
                   poolw_ref, pscale_ref,
                   outa_ref, outb_ref, q_ref, k_ref, v_ref, gi_ref, gf_ref, vn_ref, pool_ref,
                   pooled_scr):
    h, h_pieces = _prenorm(x_ref[...], gpre_ref[...])
    _branch_a(h, h_pieces, wcat_ref, lng_ref, lnb_ref, mixm_ref, mixb_ref, outa_ref, vn_out_ref=vn_ref)

    def hist_out(s, val):
        pool_ref[s] = val

    b_in = _proj(h, wcat_ref, OFF_B, 512)
    q, k, v = _qkv(h, wcat_ref)
    _branch_b(b_in, poolw_ref, pscale_ref, lambda s: hist_ref[s], hist_out, pooled_scr, outb_ref,
              rs=SAMPLE_SEQS_PER_BLOCK, pos1=None)
    for j in range(512 // 128):
        q_ref[j] = q[:, j * 128:(j + 1) * 128]
        k_ref[j] = k[:, j * 128:(j + 1) * 128]
    for j in range(D_MODEL // 128):
        v_ref[j] = v[:, j * 128:(j + 1) * 128]
    gi_ref[...] = _dot(h, wifc_ref[:, 0:HEADS * DQK])
    gf_ref[...] = _dot(h, wifc_ref[:, HEADS * DQK:2 * HEADS * DQK])


def _mix_s2_kernel(q_ref, k_ref, v_ref, gi_ref, gf_ref, ib_ref, fb_ref, normg_ref, c_ref, n_ref, m_ref,
                   cbuf_ref, hc_ref, cn_ref, nn_ref, mn_ref,
                   cq_scr, win_scr, dec_scr):
    del cbuf_ref
    nb = SAMPLE_SEQS_PER_BLOCK
    steps = SAMPLE_STEPS

    vblocks = DV // 128

    def slab(ref, t):
        return ref[t * nb:(t + 1) * nb, :]

    def slab_v(ref, hd, t):
        return jnp.concatenate([ref[hd * vblocks + j, t * nb:(t + 1) * nb, :] for j in range(vblocks)], axis=1)

    def twice(x):
        return jnp.concatenate([x] * vblocks, axis=1)

    m0 = m_ref[...]
    ig = [slab(gi_ref, t) + ib_ref[...] for t in range(steps)]
    lf = [_log_sigmoid(slab(gf_ref, t) + fb_ref[...]) for t in range(steps)]
    b, a, m_t = [], [], []
    run_b = jnp.zeros_like(m0)
    run_max = jnp.full(m0.shape, -jnp.inf, F32)
    for t in range(steps):
        run_b = run_b + lf[t]
        b.append(run_b)
        a.append(ig[t] - run_b)
        run_max = jnp.maximum(run_max, a[t])
        m_t.append(run_b + jnp.maximum(m0, run_max))
    inter = [jnp.exp(b[t] + m0 - m_t[t]) for t in range(steps)]
    m_end = m_t[-1]
    decay = jnp.exp(b[-1] + m0 - m_end)
    w_in = [jnp.exp(a[t] + b[-1] - m_end) for t in range(steps)]
    mn_ref[...] = m_end
    for hd in range(HEADS):
        hcols = slice(hd * DQK, (hd + 1) * DQK)
        for t in range(steps):
            win_scr[hd, t * nb:(t + 1) * nb, :] = w_in[t][:, hcols]
            dec_scr[hd, t * nb:(t + 1) * nb, :] = decay[:, hcols]

    def seq_body(bi, carry):
        rows = pl.ds(bi, steps, stride=nb)
        for hd in range(HEADS):
            c_old = c_ref[bi, hd]
            qb = q_ref[hd, rows, :].astype(BF16)
            cq = _dot_nt(qb, c_old.astype(BF16))
            for j in range(vblocks):
                cq_scr[hd * vblocks + j, rows, :] = cq[:, j * 128:(j + 1) * 128]
            vrows = jnp.concatenate([v_ref[hd * vblocks + j, rows, :] for j in range(vblocks)], axis=1)
            vs = (vrows * twice(win_scr[hd, rows, :])).astype(BF16)
            kb = k_ref[hd, rows, :].astype(BF16)
            dec_row = dec_scr[hd, rows, :][0:1, :]
            cn_ref[bi, hd] = dec_row * c_old + _dot_tn(vs, kb)
        return carry

    lax.fori_loop(0, nb, seq_body, 0, unroll=SAMPLE_SEQ_UNROLL)

    pairs = [(t, s) for t in range(steps) for s in range(t + 1)]
    dfac = [jnp.exp((b[t] - m_t[t]) + a[s]) for (t, s) in pairs]
    enm = [jnp.exp(-m_t[t]) for t in range(steps)]
    for hd in range(HEADS):
        hcols = slice(hd * DQK, (hd + 1) * DQK)
        vcols = slice(hd * DV, (hd + 1) * DV)
        qs = [slab(q_ref.at[hd], t) for t in range(steps)]
        ks = [slab(k_ref.at[hd], t) for t in range(steps)]
        vs = [slab_v(v_ref, hd, t) for t in range(steps)]
        n_old = n_ref[:, hcols]
        n_new = decay[:, hcols] * n_old
        for t in range(steps):
            n_new = n_new + w_in[t][:, hcols] * ks[t]
        nn_ref[:, hcols] = n_new
        prods = jnp.concatenate([qs[t] * ks[s] for (t, s) in pairs] + [qs[t] * n_old for t in range(steps)], axis=0)
        red = jnp.sum(prods, axis=1, keepdims=True)
        wts = red[:len(pairs) * nb] * jnp.concatenate([d[:, hcols] for d in dfac], axis=0)
        for t in range(steps):
            inter_t = inter[t][:, hcols]
            num = twice(inter_t) * slab_v(cq_scr, hd, t)
            den = inter_t * red[(len(pairs) + t) * nb:(len(pairs) + t + 1) * nb]
            for s in range(t + 1):
                p = pairs.index((t, s))
                w_ts = wts[p * nb:(p + 1) * nb]
                num = num + twice(w_ts) * vs[s]
                den = den + w_ts
            hout = num / twice(jnp.maximum(jnp.abs(den), enm[t][:, hcols]))
            hc_ref[t * nb:(t + 1) * nb, vcols] = _rms(hout, normg_ref[:, vcols])


def _mix_s3_kernel(x_ref, outa_ref, outb_ref, hc_ref, gpre_ref, wgate_ref, pa_ref, pb_ref, pc_ref,
                   wout_ref, gpost_ref, o_ref):
    x = x_ref[...]
    h, h_pieces = _prenorm(x, gpre_ref[...])

    def gate(j):
        pieces = h_pieces if j == 0 else [h]
        return _sigmoid(_rows_dot(pieces, _gate_weight(j, D_MODEL, wgate_ref.at[0], col0=OFF_O)))

    o_ref[...] = _merge(x, gate, hc_ref[...], pa_ref, pb_ref, pc_ref, wout_ref, gpost_ref, outa_ref, outb_ref)


def _mix_sample(xs, hist, c_all, c_buf, n_in, m_in, lw, layer):
    rows = xs.shape[0]
    nblk = rows // BLOCK_ROWS
    hist_rows = POOL_HIST * SAMPLE_SEQS_PER_BLOCK
    cs1 = functools.partial(_const_spec, index=layer, ngrid=1)
    full = lambda shape: pl.BlockSpec(shape, lambda i: (0,) * len(shape))
    params = pltpu.CompilerParams(dimension_semantics=("arbitrary",), vmem_limit_bytes=VMEM_LIMIT_BYTES)

    w1 = [lw['mix_norm_pre'], lw['wcat'], lw['wifc'], lw['ln_g'], lw['ln_b'], lw['mixm_s'], lw['mixb_s'],
          lw['pool_w'], lw['pool_scale']]
    s1_specs = [cs1(w.shape) for w in w1]
    s1_specs[1] = pl.BlockSpec((None, D_MODEL, OFF_O), lambda i: (layer, 0, 0), pipeline_mode=pl.Buffered(1))
    s1_shapes = (
        jax.ShapeDtypeStruct((rows, 512), BF16),
        jax.ShapeDtypeStruct((rows, 512), BF16),
        jax.ShapeDtypeStruct((HEADS, rows, DQK), F32),
        jax.ShapeDtypeStruct((HEADS, rows, DQK), F32),
        jax.ShapeDtypeStruct((D_MODEL // 128, rows, 128), F32),
        jax.ShapeDtypeStruct((rows, HEADS * DQK), F32),
        jax.ShapeDtypeStruct((rows, HEADS * DQK), F32),
        jax.ShapeDtypeStruct((rows, 512), F32),
        jax.ShapeDtypeStruct((nblk, hist_rows, 512), F32),
    )
    outa, outb, q, k, v, gi, gf, vn, pool_new = pl.pallas_call(
        _mix_s1_kernel,
        grid=(1,),
        in_specs=[full(xs.shape), full(hist.shape)] + s1_specs,
        out_specs=tuple(full(s.shape) for s in s1_shapes),
        out_shape=s1_shapes,
        scratch_shapes=[pltpu.VMEM((rows, 512), BF16)],
        compiler_params=params,
        name="mix_s1",
    )(xs, hist, *w1)

    nb = SAMPLE_SEQS_PER_BLOCK
    blk = lambda width: pl.BlockSpec((BLOCK_ROWS, width), lambda j: (j, 0))
    seqblk = lambda width: pl.BlockSpec((nb, width), lambda j: (j, 0))
    cblk = pl.BlockSpec((None, nb, HEADS, DV, DQK), lambda j: (layer, j, 0, 0, 0))
    lrow = lambda width: pl.BlockSpec((None, 1, width), lambda j: (layer, 0, 0))
    colblk = lambda n: pl.BlockSpec((n, BLOCK_ROWS, 128), lambda j: (0, j, 0))
    hc, c_buf, n_new, m_new = pl.pallas_call(
        _mix_s2_kernel,
        grid=(nblk,),
        in_specs=[colblk(HEADS), colblk(HEADS), colblk(D_MODEL // 128), blk(512), blk(512),
                  lrow(512), lrow(512), lrow(D_MODEL), cblk, seqblk(512), seqblk(512),
                  pl.BlockSpec(memory_space=pl.ANY)],
        out_specs=(blk(D_MODEL), cblk, seqblk(512), seqblk(512)),
        out_shape=(jax.ShapeDtypeStruct((rows, D_MODEL), F32),
                   jax.ShapeDtypeStruct(c_all.shape, F32),
                   jax.ShapeDtypeStruct(n_in.shape, F32),
                   jax.ShapeDtypeStruct(m_in.shape, F32)),
        input_output_aliases={11: 1},
        scratch_shapes=[pltpu.VMEM((D_MODEL // 128, BLOCK_ROWS, 128), F32),
                        pltpu.VMEM((HEADS, BLOCK_ROWS, 128), F32), pltpu.VMEM((HEADS, BLOCK_ROWS, 128), F32)],
        compiler_params=pltpu.CompilerParams(dimension_semantics=("parallel",),
                                             vmem_limit_bytes=VMEM_LIMIT_BYTES),
        name="mix_s2",
    )(q, k, v, gi, gf, lw['ib'], lw['fb'], lw['norm_g'], c_all, n_in, m_in, c_buf)

    w3 = [lw['mix_norm_pre'], lw['wcat'], lw['proj_a'], lw['proj_b'], lw['proj_c'], lw['w_out'],
          lw['mix_norm_post']]
    s3_specs = [cs1(w.shape) for w in w3]
    s3_specs[1] = pl.BlockSpec((pl.Element(1), pl.Element(D_MODEL), pl.Element(WT_ROWS - OFF_O)),
                               lambda i: (layer, 0, OFF_O), pipeline_mode=pl.Buffered(1))
    y = pl.pallas_call(
        _mix_s3_kernel,
        grid=(1,),
        in_specs=[full(xs.shape), full(outa.shape), full(outb.shape), full(hc.shape)] + s3_specs,
        out_specs=full(xs.shape),
        out_shape=jax.ShapeDtypeStruct(xs.shape, F32),
        compiler_params=params,
        name="mix_s3",
    )(xs, outa, outb, hc, *w3)
    return y, vn, pool_new, c_buf, n_new, m_new


def _to_blocked(x):
    nseq, t, c = x.shape
    nb = SAMPLE_SEQS_PER_BLOCK
    return x.reshape(nseq // nb, nb, t, c).transpose(0, 2, 1, 3).reshape(nseq * t, c)


def _from_blocked(x, t):
    rows, c = x.shape
    nb = SAMPLE_SEQS_PER_BLOCK
    nseq = rows // t
    return x.reshape(nseq // nb, t, nb, c).transpose(0, 2, 1, 3).reshape(nseq, t, c)


def _cast_cols_kernel(wt_ref, o_ref):
    o_ref[...] = wt_ref[0].T.astype(BF16)


def _input_proj_weights(w_in):
    w_t = jnp.swapaxes(w_in, 1, 2)
    rows = 512
    src_row = lambda l, r: (l, pl.multiple_of(r * rows + jnp.where(r * rows >= OFF_G, GATE_COLS, 0), 8), 0)
    return pl.pallas_call(
        _cast_cols_kernel,
        grid=(DEPTH, WT_ROWS // rows),
        in_specs=[pl.BlockSpec((pl.Element(1), pl.Element(rows), pl.Element(D_MODEL)), src_row)],
        out_specs=pl.BlockSpec((None, D_MODEL, rows), lambda l, r: (l, 0, r)),
        out_shape=jax.ShapeDtypeStruct((DEPTH, D_MODEL, WT_ROWS), BF16),
        compiler_params=pltpu.CompilerParams(
            dimension_semantics=("parallel", "parallel"), vmem_limit_bytes=VMEM_LIMIT_BYTES),
        name="cast_w_in",
    )(w_t)


def _prep_weights(p):
    w_in = p['w_in']
    row = lambda a: a.reshape(DEPTH, 1, -1).astype(F32)
    wi = w_in[:, :, OFF_G:OFF_G + HEADS]
    wf = w_in[:, :, OFF_G + HEADS:OFF_G + 2 * HEADS]
    wift = jnp.zeros((DEPTH, 32, D_MODEL), F32)
    wift = wift.at[:, 0:HEADS].set(wi.transpose(0, 2, 1)).at[:, 16:16 + HEADS].set(wf.transpose(0, 2, 1))
    wifc = jnp.concatenate([jnp.repeat(wi, DQK, axis=2), jnp.repeat(wf, DQK, axis=2)], axis=2)
    gbias = jnp.zeros((DEPTH, 32, BLOCK_ROWS), F32)
    gbias = gbias.at[:, 0:HEADS].set(jnp.broadcast_to(p['mlstm_i_bias'][:, :, None], (DEPTH, HEADS, BLOCK_ROWS)))
    gbias = gbias.at[:, 16:16 + HEADS].set(
        jnp.broadcast_to(p['mlstm_f_bias'][:, :, None], (DEPTH, HEADS, BLOCK_ROWS)))
    lane_pad = lambda a: jnp.repeat(a.astype(F32), DQK, axis=1).reshape(DEPTH, 1, HEADS * DQK)

    ws = p['gmlp_w_s']
    bs = p['gmlp_b_s']
    mixm_p = jnp.tril(ws)
    mixb_p = jnp.broadcast_to(bs[..., None], ws.shape)
    nb = SAMPLE_SEQS_PER_BLOCK
    tril_s = jnp.tril(ws[:, :, :SAMPLE_STEPS, :SAMPLE_STEPS])
    eye = jnp.eye(nb, dtype=F32)
    mixm_s = jnp.einsum('lgts,bc->lgtbsc', tril_s, eye).reshape(DEPTH, N_GROUPS, BLOCK_ROWS, BLOCK_ROWS)
    mixb_s = jnp.broadcast_to(jnp.repeat(bs[:, :, :SAMPLE_STEPS], nb, axis=2)[..., None], ws.shape)

    return dict(
        ffn1_norm_pre=row(p['ffn1_norm_pre']), ffn1_w_in=p['ffn1_w_in'].astype(BF16),
        ffn1_w_down=p['ffn1_w_down'].astype(BF16), ffn1_norm_post=row(p['ffn1_norm_post']),
        ffn2_norm_pre=row(p['ffn2_norm_pre']), ffn2_w_in=p['ffn2_w_in'].astype(BF16),
        ffn2_w_down=p['ffn2_w_down'].astype(BF16), ffn2_norm_post=row(p['ffn2_norm_post']),
        mix_norm_pre=row(p['mix_norm_pre']), mix_norm_post=row(p['mix_norm_post']),
        wcat=_input_proj_weights(w_in),
        wift=wift.astype(BF16), wifc=wifc.astype(BF16), gbias=gbias,
        ib=lane_pad(p['mlstm_i_bias']), fb=lane_pad(p['mlstm_f_bias']),
        ln_g=row(p['gmlp_ln_g']), ln_b=row(p['gmlp_ln_b']),
        mixm_p=mixm_p.astype(BF16), mixb_p=mixb_p.astype(F32),
        mixm_s=mixm_s.astype(BF16), mixb_s=mixb_s.astype(F32),
        pool_w=p['pool_w'].astype(BF16), pool_scale=row(p['pool_scale']), norm_g=row(p['mlstm_norm_g']),
        proj_a=p['proj_a'].astype(BF16), proj_b=p['proj_b'].astype(BF16), proj_c=p['proj_c'].astype(BF16),
        w_out=p['w_out'].astype(BF16),
    )


def kernel(x_prompt, x_sample, state_pool, state_mlstm_C, state_mlstm_n, state_mlstm_m, ffn1_norm_pre, ffn1_w_in, ffn1_w_down, ffn1_norm_post, mix_norm_pre, w_in, gmlp_ln_g, gmlp_ln_b, gmlp_w_s, gmlp_b_s, pool_w, pool_scale, mlstm_i_bias, mlstm_f_bias, mlstm_norm_g, proj_a, proj_b, proj_c, w_out, mix_norm_post, ffn2_norm_pre, ffn2_w_in, ffn2_w_down, ffn2_norm_post):
    params = dict(ffn1_norm_pre=ffn1_norm_pre, ffn1_w_in=ffn1_w_in, ffn1_w_down=ffn1_w_down,
                  ffn1_norm_post=ffn1_norm_post, mix_norm_pre=mix_norm_pre, w_in=w_in, gmlp_ln_g=gmlp_ln_g,
                  gmlp_ln_b=gmlp_ln_b, gmlp_w_s=gmlp_w_s, gmlp_b_s=gmlp_b_s, pool_w=pool_w,
                  pool_scale=pool_scale, mlstm_i_bias=mlstm_i_bias, mlstm_f_bias=mlstm_f_bias,
                  mlstm_norm_g=mlstm_norm_g, proj_a=proj_a, proj_b=proj_b, proj_c=proj_c, w_out=w_out,
                  mix_norm_post=mix_norm_post, ffn2_norm_pre=ffn2_norm_pre, ffn2_w_in=ffn2_w_in,
                  ffn2_w_down=ffn2_w_down, ffn2_norm_post=ffn2_norm_post)
    lw = _prep_weights(params)
    nbp, seq, _ = x_prompt.shape
    nbs, steps, _ = x_sample.shape

    yp = x_prompt
    ys = _to_blocked(x_sample)
    hist_all = jnp.pad(state_pool, ((0, 0), (0, 0), (1, 0), (0, 0)))
    n_all = state_mlstm_n.reshape(DEPTH, nbs, HEADS * DQK)
    m_all = jnp.repeat(state_mlstm_m, DQK, axis=2)

    pool_p, c_p, n_p, m_p = [], [], [], []
    pool_s, n_s, m_s, v_s = [], [], [], []
    c_sample = jnp.zeros(state_mlstm_C.shape, F32)
    for l in range(DEPTH):
        ffn1 = (lw['ffn1_norm_pre'], lw['ffn1_w_in'], lw['ffn1_w_down'], lw['ffn1_norm_post'])
        ffn2 = (lw['ffn2_norm_pre'], lw['ffn2_w_in'], lw['ffn2_w_down'], lw['ffn2_norm_post'])
        yp = _ffn(yp.reshape(nbp * seq, D_MODEL), *ffn1, l).reshape(nbp, seq, D_MODEL)
        yp, pb, cf, nf_, mf = _mix_prompt(yp, lw, l)
        yp = _ffn(yp.reshape(nbp * seq, D_MODEL), *ffn2, l).reshape(nbp, seq, D_MODEL)
        pool_p.append(pb[:, 1:, :])
        c_p.append(cf)
        n_p.append(nf_[:, :HEADS, :])
        m_p.append(mf[:, :HEADS, 0])
        ys = _ffn(ys, *ffn1, l)
        hist = _to_blocked(hist_all[l]).reshape(nbs // SAMPLE_SEQS_PER_BLOCK,
                                                POOL_HIST * SAMPLE_SEQS_PER_BLOCK, 512)
        ys, vn, pool_new, c_sample, n_new, m_new = _mix_sample(
            ys, hist, state_mlstm_C, c_sample, n_all[l], m_all[l], lw, l)
        ys = _ffn(ys, *ffn2, l)
        pool_s.append(_from_blocked(pool_new.reshape(nbs * POOL_HIST, 512), POOL_HIST)[:, 1:, :])
        n_s.append(n_new.reshape(nbs, HEADS, DQK))
        m_s.append(m_new.reshape(nbs, HEADS, DQK)[:, :, 0])
        v_s.append(_from_blocked(vn, steps))
    return (yp, _from_blocked(ys, steps),
            jnp.stack(pool_p), jnp.stack(c_p), jnp.stack(n_p), jnp.stack(m_p),
            jnp.stack(pool_s), c_sample, jnp.stack(n_s), jnp.stack(m_s), jnp.stack(v_s))
```

```python
import functools

import jax
import jax.numpy as jnp
from jax import lax
from jax.experimental import pallas as pl
from jax.experimental.pallas import tpu as pltpu

F32 = jnp.float32
BF16 = jnp.bfloat16

D_MODEL = 1024
D_FF = 2816
DEPTH = 4
EPS = 1e-6
PAST_LEN = 16384
N_GROUPS = 4
GROUP_W = 128
POOL_WINDOWS = (2, 4, 8, 16)
POOL_HIST = 16
HEADS = 4
DQK = 128
DV = 256
BLOCK_ROWS = 128
SAMPLE_SEQS_PER_BLOCK = 16
SAMPLE_STEPS = 8

OFF_AU, OFF_AV, OFF_B, OFF_Q, OFF_K, OFF_V, OFF_O = 0, 512, 1024, 1536, 2048, 2560, 3584
GATE_COLS = 8
OFF_G = 4608
WT_ROWS = OFF_G + 3 * D_MODEL

SAMPLE_SEQ_UNROLL = 4

FFN_TM = 2048
FFN_TF = 256
CAST_COLS = 512
ZERO_BLOCKS = 64
MIX_ROW_SPLIT = 2
FFN_ROW_BLOCKS = 4
PROMPT_SEQS_PER_STEP = 4
VMEM_LIMIT_BYTES = 56 * 1024 * 1024


def _dot(a, b):
    return jnp.dot(a, b, preferred_element_type=F32)


def _dot_nt(a, b):
    return lax.dot_general(a, b, (((1,), (1,)), ((), ())), preferred_element_type=F32)


def _dot_tn(a, b):
    return lax.dot_general(a, b, (((0,), (0,)), ((), ())), preferred_element_type=F32)


def _rms(x, g):
    return x * lax.rsqrt(jnp.mean(x * x, axis=-1, keepdims=True) + EPS) * g


def _sigmoid(x):
    return 1.0 / (1.0 + jnp.exp(-x))


def _gelu(x):
    return 0.5 * x * (1.0 + jnp.tanh(0.7978845608028654 * (x + 0.044715 * (x * x * x))))


def _log_sigmoid(x):
    return jnp.minimum(x, 0.0) - jnp.log1p(jnp.exp(-jnp.abs(x)))


def _const_spec(shape, index, ngrid):
    zeros = (0,) * (len(shape) - 1)
    if ngrid == 1:
        imap = lambda i: (index,) + zeros
    else:
        imap = lambda i, j: (index,) + zeros
    return pl.BlockSpec((None,) + tuple(shape[1:]), imap, pipeline_mode=pl.Buffered(1))


def _ffn_tiles(m):
    if m > FFN_TM:
        return FFN_TM, FFN_TF, FFN_ROW_BLOCKS
    return m, D_FF // 2, FFN_ROW_BLOCKS // 2


def _ffn_kernel(x_ref, gpre_ref, wgate_ref, wup_ref, wdown_ref, gpost_ref, *rest, row_blocks, nf, side):
    if side == 'cast':
        wt_ref, o_ref, side_ref, h_scr = rest
    elif side == 'zeros':
        o_ref, side_ref, h_scr = rest
    else:
        o_ref, h_scr = rest

    def side_job():
        if side == 'cast':
            side_ref[...] = wt_ref[0].T.astype(BF16)
        elif side == 'zeros':
            side_ref[...] = jnp.zeros(side_ref.shape, F32)

    f = pl.program_id(1)
    last = nf - 1
    tm = x_ref.shape[0]
    rb = tm // row_blocks

    def step(first, final):
        wgate = wgate_ref[...]
        wup = wup_ref[...]
        wdown = wdown_ref[...]
        side_job()
        for r in range(row_blocks):
            rows = slice(r * rb, (r + 1) * rb)
            if first:
                h = _rms(x_ref[rows, :], gpre_ref[...]).astype(BF16)
                h_scr[rows, :] = h
            else:
                h = h_scr[rows, :]
            gate = _dot(h, wgate)
            up = _dot(h, wup)
            act = (gate * _sigmoid(gate) * up).astype(BF16)
            part = _dot(act, wdown)
            acc = part if first else o_ref[rows, :] + part
            if final:
                acc = x_ref[rows, :] + _rms(acc, 0.5 * gpost_ref[...])
            o_ref[rows, :] = acc

    @pl.when(f == 0)
    def _():
        step(True, False)

    if nf > 2:
        @pl.when(jnp.logical_and(f > 0, f < last))
        def _():
            step(False, False)

    @pl.when(f == last)
    def _():
        step(False, True)


def _ffn(x2d, gpre, w_in, wdown, gpost, layer, side=None, side_arg=None):
    m = x2d.shape[0]
    tm, tf, row_blocks = _ffn_tiles(m)
    nf = D_FF // tf
    in_specs = [
        pl.BlockSpec((tm, D_MODEL), lambda i, f: (i, 0)),
        pl.BlockSpec((None, 1, D_MODEL), lambda i, f: (layer, 0, 0)),
        pl.BlockSpec((None, D_MODEL, tf), lambda i, f: (layer, 0, f)),
        pl.BlockSpec((None, D_MODEL, tf), lambda i, f: (layer, 0, f + nf)),
        pl.BlockSpec((None, tf, D_MODEL), lambda i, f: (layer, f, 0)),
        pl.BlockSpec((None, 1, D_MODEL), lambda i, f: (layer, 0, 0)),
    ]
    operands = [x2d, gpre, w_in, w_in, wdown, gpost]
    out_specs = [pl.BlockSpec((tm, D_MODEL), lambda i, f: (i, 0))]
    out_shape = [jax.ShapeDtypeStruct((m, D_MODEL), F32)]
    steps = (m // tm) * nf
    if side == 'cast':
        per_layer = WT_ROWS // CAST_COLS
        assert steps >= DEPTH * per_layer
        blk = lambda i, f: jnp.minimum(i * nf + f, DEPTH * per_layer - 1)
        col = lambda i, f: (blk(i, f) % per_layer) * CAST_COLS

        def src(i, f):
            row = col(i, f) + jnp.where(col(i, f) >= OFF_G, GATE_COLS, 0)
            return (blk(i, f) // per_layer, pl.multiple_of(row, 8), 0)

        in_specs.append(pl.BlockSpec((pl.Element(1), pl.Element(CAST_COLS), pl.Element(D_MODEL)), src))
        operands.append(side_arg)
        out_specs.append(pl.BlockSpec((None, D_MODEL, CAST_COLS),
                                      lambda i, f: (blk(i, f) // per_layer, 0, blk(i, f) % per_layer)))
        out_shape.append(jax.ShapeDtypeStruct((DEPTH, D_MODEL, WT_ROWS), BF16))
    elif side == 'zeros':
        nblk, blk_rows = side_arg
        assert steps >= nblk
        out_specs.append(pl.BlockSpec((None, blk_rows, 128), lambda i, f: (jnp.minimum(i * nf + f, nblk - 1), 0, 0)))
        out_shape.append(jax.ShapeDtypeStruct((nblk, blk_rows, 128), F32))
    out = pl.pallas_call(
        functools.partial(_ffn_kernel, row_blocks=row_blocks, nf=nf, side=side),
        grid=(m // tm, nf),
        in_specs=in_specs,
        out_specs=tuple(out_specs),
        out_shape=tuple(out_shape),
        scratch_shapes=[pltpu.VMEM((tm, D_MODEL), BF16)],
        compiler_params=pltpu.CompilerParams(
            dimension_semantics=("arbitrary", "arbitrary"), vmem_limit_bytes=VMEM_LIMIT_BYTES),
        name="ffn",
    )(*operands)
    return out[0] if side is None else out


def _prenorm(x, g):
    step = x.shape[0] // MIX_ROW_SPLIT
    pieces = [_rms(x[i * step:(i + 1) * step, :], g).astype(BF16) for i in range(MIX_ROW_SPLIT)]
    return jnp.concatenate(pieces, axis=0), pieces


def _proj(h, w_ref, lo, n):
    return _dot(h, w_ref[:, lo:lo + n])


def _rows_dot(pieces, w):
    return jnp.concatenate([_dot(p, w) for p in pieces], axis=0)


def _branch_a(h, h_pieces, wcat_ref, lng_ref, lnb_ref, mixm_ref, mixb_ref, outa_scr, vn_out_ref=None):
    rows = h.shape[0]
    a_v = _gelu(_rows_dot(h_pieces, wcat_ref[:, OFF_AV:OFF_AV + 512]))
    mu = jnp.mean(a_v, axis=-1, keepdims=True)
    xc = a_v - mu
    var = jnp.mean(xc * xc, axis=-1, keepdims=True)
    v_n = xc * lax.rsqrt(var + EPS) * lng_ref[...] + lnb_ref[...]
    if vn_out_ref is not None:
        vn_out_ref[...] = v_n
    vb = v_n.astype(BF16)
    a_u = _gelu(_proj(h, wcat_ref, OFF_AU, 512))
    for g in range(N_GROUPS):
        cols = slice(g * GROUP_W, (g + 1) * GROUP_W)
        mix_g = mixm_ref[g]
        bias_g = mixb_ref[g]
        for s in range(rows // BLOCK_ROWS):
            rws = slice(s * BLOCK_ROWS, (s + 1) * BLOCK_ROWS)
            mixed = _dot(mix_g, vb[rws, cols]) + bias_g
            outa_scr[rws, cols] = (a_u[rws, cols] * mixed).astype(BF16)


def _branch_b(b_in, poolw_ref, pscale_ref, hist_in, hist_out, pooled_scr, outb_scr, *, rs, pos1):
    rows = b_in.shape[0]
    hist_rows = POOL_HIST * rs
    for s in range(rows // BLOCK_ROWS):
        rws = slice(s * BLOCK_ROWS, (s + 1) * BLOCK_ROWS)
        cur = b_in[rws, :]
        ext = jnp.concatenate([hist_in(s), cur], axis=0)
        for g, w in enumerate(POOL_WINDOWS):
            cols = slice(g * GROUP_W, (g + 1) * GROUP_W)
            acc = ext[:, cols]
            sh = 1
            while sh < w:
                acc = acc + pltpu.roll(acc, sh * rs, axis=0)
                sh *= 2
            win = acc[hist_rows:, :]
            if pos1 is None:
                mean = win * (1.0 / w)
            else:
                mean = win / jnp.minimum(pos1, w).astype(F32)
            pooled_scr[rws, cols] = (mean - cur[:, cols]).astype(BF16)
        hist_out(s, ext[BLOCK_ROWS:, :])
    for g in range(N_GROUPS):
        cols = slice(g * GROUP_W, (g + 1) * GROUP_W)
        mixed = _dot(pooled_scr[:, cols], poolw_ref[g]) * pscale_ref[:, cols]
        outb_scr[:, cols] = mixed.astype(BF16)


def _gate_weight(j, width, wcat_ref, col0=0):
    lo = j * width
    lo = OFF_G + lo if lo < 3 * D_MODEL else OFF_O + lo - 3 * D_MODEL
    return wcat_ref[:, lo - col0:lo - col0 + width]


def _merge(x, gate, hc, pa_ref, pb_ref, pc_ref, wout_ref, gpost_ref, outa_scr, outb_scr):
    merged = gate(0) * _dot(outa_scr[...], pa_ref[...])
    merged += gate(1) * _dot(outb_scr[...], pb_ref[...])
    out_c = (gate(3) * hc).astype(BF16)
    merged += gate(2) * _dot(out_c, pc_ref[...])
    mb = merged.astype(BF16)
    step = x.shape[0] // MIX_ROW_SPLIT
    outs = []
    for i in range(MIX_ROW_SPLIT):
        rws = slice(i * step, (i + 1) * step)
        outs.append(x[rws, :] + _rms(_dot(mb[rws, :], wout_ref[...]), gpost_ref[...]))
    return jnp.concatenate(outs, axis=0)


def _scan_lanes(x, op, fill):
    lane = lax.broadcasted_iota(jnp.int32, x.shape, 1)
    sh = 1
    while sh < x.shape[1]:
        x = op(x, jnp.where(lane >= sh, pltpu.roll(x, sh, axis=1), fill))
        sh *= 2
    return x


def _last_lane(x):
    lane = lax.broadcasted_iota(jnp.int32, x.shape, 1)
    return jnp.max(jnp.where(lane == x.shape[1] - 1, x, -jnp.inf), axis=1, keepdims=True)


def _mlstm_gates(h_pieces, wift_ref, gbias_ref, m_ref, *, nseq):
    t_len = BLOCK_ROWS
    grow = jnp.concatenate([_dot_nt(wift_ref[...], p) for p in h_pieces], axis=1)
    seqs = range(nseq)
    ig = jnp.concatenate([grow[0:16, s * t_len:(s + 1) * t_len] + gbias_ref[0:16, :] for s in seqs], axis=0)
    lf = _log_sigmoid(
        jnp.concatenate([grow[16:32, s * t_len:(s + 1) * t_len] + gbias_ref[16:32, :] for s in seqs], axis=0))
    m0 = m_ref[...].reshape(nseq * 16, t_len)
    b = _scan_lanes(lf, jnp.add, 0.0)
    a = ig - b
    m_t = b + jnp.maximum(m0, _scan_lanes(a, jnp.maximum, -jnp.inf))
    inter = jnp.exp(b + m0 - m_t)
    b_last = _last_lane(b)
    m_end = _last_lane(m_t)
    decay = jnp.exp(b_last + m0 - m_end)
    w_in = jnp.exp(a + b_last - m_end)
    m_ref[...] = jnp.broadcast_to(m_end, (nseq * 16, t_len)).reshape(nseq, 16, t_len)
    stacked = jnp.concatenate([b - m_t, inter, jnp.exp(-m_t), w_in], axis=0)
    pad = (-stacked.shape[0]) % t_len
    if pad:
        stacked = jnp.concatenate([stacked, jnp.zeros((pad, t_len), F32)], axis=0)
    cols_t = [stacked[i * t_len:(i + 1) * t_len, :].T for i in range(stacked.shape[0] // t_len)]

    def col(quantity, s, hd):
        idx = quantity * nseq * 16 + s * 16 + hd
        return cols_t[idx // t_len][:, idx % t_len:idx % t_len + 1]

    return a, decay, col


def _qkv(h, wcat_ref):
    q = _proj(h, wcat_ref, OFF_Q, 512)
    k = _proj(h, wcat_ref, OFF_K, 512) * (DQK ** -0.5)
    v = _proj(h, wcat_ref, OFF_V, D_MODEL)
    return q, k, v


def _mlstm_prompt(qkv, gates, normg_ref, c_ref, n_ref, hc_scr, *, nseq, fillers):
    t_len = BLOCK_ROWS
    a_all, decay_all, col = gates
    q, k, v = qkv
    ri = lax.broadcasted_iota(jnp.int32, (t_len, t_len), 0)
    ci = lax.broadcasted_iota(jnp.int32, (t_len, t_len), 1)
    causal = ci <= ri
    for s in range(nseq):
        rws = slice(s * t_len, (s + 1) * t_len)
        a = a_all[s * 16:(s + 1) * 16, :]
        decay = decay_all[s * 16:(s + 1) * 16, :]
        for hd in range(HEADS):
            fillers[s * HEADS + hd]()
            qh = q[rws, hd * DQK:(hd + 1) * DQK]
            kh = k[rws, hd * DQK:(hd + 1) * DQK]
            vh = v[rws, hd * DV:(hd + 1) * DV]
            qb = qh.astype(BF16)
            kb = kh.astype(BF16)
            bm_c = col(0, s, hd)
            inter_c = col(1, s, hd)
            en_c = col(2, s, hd)
            win_c = col(3, s, hd)
            dmat = jnp.exp(jnp.where(causal, bm_c + a[hd:hd + 1, :], -jnp.inf))
            wmat = _dot_nt(qb, kb) * dmat
            c_old = c_ref[s, hd]
            n_old = n_ref[s, hd:hd + 1, :]
            num = _dot(wmat.astype(BF16), vh.astype(BF16)) + inter_c * _dot_nt(qb, c_old.astype(BF16))
            den = jnp.sum(wmat, axis=1, keepdims=True) + inter_c * jnp.sum(qh * n_old, axis=1, keepdims=True)
            hout = num / jnp.maximum(jnp.abs(den), en_c)
            hc_scr[rws, hd * DV:(hd + 1) * DV] = _rms(hout, normg_ref[:, hd * DV:(hd + 1) * DV])
            dec_row = decay[hd:hd + 1, :]
            c_ref[s, hd] = dec_row * c_old + _dot_tn((vh * win_c).astype(BF16), kb)
            n_ref[s, hd:hd + 1, :] = dec_row * n_old + jnp.sum(kh * win_c, axis=0, keepdims=True)


def _mix_prompt_kernel(x_ref, gpre_ref, wcat_ref, wift_ref, gbias_ref, lng_ref, lnb_ref,
                       mixm_ref, mixb_ref, poolw_ref, pscale_ref, normg_ref, pa_ref, pb_ref, pc_ref, wout_ref,
                       gpost_ref,
                       o_ref, pool_ref, c_ref, n_ref, m_ref,
                       outa_scr, outb_scr, pooled_scr, hc_scr, gate_scr, *, nseq):
    c = pl.program_id(1)
    rows = nseq * BLOCK_ROWS

    @pl.when(c == 0)
    def _():
        pool_ref[...] = jnp.zeros(pool_ref.shape, F32)
        c_ref[...] = jnp.zeros(c_ref.shape, F32)
        n_ref[...] = jnp.zeros(n_ref.shape, F32)
        m_ref[...] = jnp.zeros(m_ref.shape, F32)

    x = x_ref[...].reshape(rows, D_MODEL)
    h, h_pieces = _prenorm(x, gpre_ref[...])

    gates = _mlstm_gates(h_pieces, wift_ref, gbias_ref, m_ref, nseq=nseq)

    _branch_a(h, h_pieces, wcat_ref, lng_ref, lnb_ref, mixm_ref, mixb_ref, outa_scr)

    b_in = _proj(h, wcat_ref, OFF_B, 512)
    qkv = _qkv(h, wcat_ref)

    pos1 = c * BLOCK_ROWS + 1 + lax.broadcasted_iota(jnp.int32, (BLOCK_ROWS, GROUP_W), 0)

    def hist_out(s, val):
        pool_ref[s] = val

    _branch_b(b_in, poolw_ref, pscale_ref, lambda s: pool_ref[s], hist_out, pooled_scr, outb_scr,
              rs=1, pos1=pos1)

    n_fill = nseq * HEADS
    fill_w = 4 * D_MODEL // n_fill

    def make_filler(j):
        def fill():
            gate_scr[:, j * fill_w:(j + 1) * fill_w] = _sigmoid(_dot(h, _gate_weight(j, fill_w, wcat_ref)))
        return fill

    _mlstm_prompt(qkv, gates, normg_ref, c_ref, n_ref, hc_scr, nseq=nseq,
                  fillers=[make_filler(j) for j in range(n_fill)])

    y = _merge(x, lambda j: gate_scr[:, j * D_MODEL:(j + 1) * D_MODEL], hc_scr[...],
               pa_ref, pb_ref, pc_ref, wout_ref, gpost_ref, outa_scr, outb_scr)
    o_ref[...] = y.reshape(nseq, BLOCK_ROWS, D_MODEL)


def _mix_prompt(x3d, lw, layer):
    nb, seq, _ = x3d.shape
    nseq = PROMPT_SEQS_PER_STEP
    rows = nseq * BLOCK_ROWS
    nchunks = seq // BLOCK_ROWS
    cs = functools.partial(_const_spec, index=layer, ngrid=2)
    weights = [lw['mix_norm_pre'], lw['wcat'], lw['wift'], lw['gbias'], lw['ln_g'], lw['ln_b'],
               lw['mixm_p'], lw['mixb_p'], lw['pool_w'], lw['pool_scale'], lw['norm_g'],
               lw['proj_a'], lw['proj_b'], lw['proj_c'], lw['w_out'], lw['mix_norm_post']]
    out_shapes = (
        jax.ShapeDtypeStruct((nb, seq, D_MODEL), F32),
        jax.ShapeDtypeStruct((nb, POOL_HIST, 512), F32),
        jax.ShapeDtypeStruct((nb, HEADS, DV, DQK), F32),
        jax.ShapeDtypeStruct((nb, 8, DQK), F32),
        jax.ShapeDtypeStruct((nb, 16, BLOCK_ROWS), F32),
    )
    out_specs = (
        pl.BlockSpec((nseq, BLOCK_ROWS, D_MODEL), lambda g, c: (g, c, 0)),
        pl.BlockSpec((nseq, POOL_HIST, 512), lambda g, c: (g, 0, 0)),
        pl.BlockSpec((nseq, HEADS, DV, DQK), lambda g, c: (g, 0, 0, 0)),
        pl.BlockSpec((nseq, 8, DQK), lambda g, c: (g, 0, 0)),
        pl.BlockSpec((nseq, 16, BLOCK_ROWS), lambda g, c: (g, 0, 0)),
    )
    return pl.pallas_call(
        functools.partial(_mix_prompt_kernel, nseq=nseq),
        grid=(nb // nseq, nchunks),
        in_specs=[pl.BlockSpec((nseq, BLOCK_ROWS, D_MODEL), lambda g, c: (g, c, 0))]
        + [cs(w.shape) for w in weights],
        out_specs=out_specs,
        out_shape=out_shapes,
        scratch_shapes=[pltpu.VMEM((rows, 512), BF16), pltpu.VMEM((rows, 512), BF16),
                        pltpu.VMEM((rows, 512), BF16), pltpu.VMEM((rows, D_MODEL), F32),
                        pltpu.VMEM((rows, 4 * D_MODEL), F32)],
        compiler_params=pltpu.CompilerParams(
            dimension_semantics=("parallel", "arbitrary"), vmem_limit_bytes=VMEM_LIMIT_BYTES),
        name="mix_prompt",
    )(x3d, *weights)


def _mix_s1_kernel(x_ref, hist_ref, gpre_ref, wcat_ref, wifc_ref, lng_ref, lnb_ref, mixm_ref, mixb_ref,
                   poolw_ref, pscale_ref,
                   outa_ref, outb_ref, q_ref, k_ref, v_ref, gi_ref, gf_ref, vn_ref, pool_ref,
                   pooled_scr):
    h, h_pieces = _prenorm(x_ref[...], gpre_ref[...])
    _branch_a(h, h_pieces, wcat_ref, lng_ref, lnb_ref, mixm_ref, mixb_ref, outa_ref, vn_out_ref=vn_ref)

    def hist_out(s, val):
        pool_ref[s] = val

    b_in = _proj(h, wcat_ref, OFF_B, 512)
    q, k, v = _qkv(h, wcat_ref)
    _branch_b(b_in, poolw_ref, pscale_ref, lambda s: hist_ref[s], hist_out, pooled_scr, outb_ref,
              rs=SAMPLE_SEQS_PER_BLOCK, pos1=None)
    for j in range(512 // 128):
        q_ref[j] = q[:, j * 128:(j + 1) * 128]
        k_ref[j] = k[:, j * 128:(j + 1) * 128]
    for j in range(D_MODEL // 128):
        v_ref[j] = v[:, j * 128:(j + 1) * 128]
    gi_ref[...] = _dot(h, wifc_ref[:, 0:HEADS * DQK])
    gf_ref[...] = _dot(h, wifc_ref[:, HEADS * DQK:2 * HEADS * DQK])


def _mix_s2_kernel(q_ref, k_ref, v_ref, gi_ref, gf_ref, ib_ref, fb_ref, normg_ref, c_ref, n_ref, m_ref,
                   cbuf_ref, hc_ref, cn_ref, nn_ref, mn_ref,
                   cq_scr, win_scr, dec_scr):
    del cbuf_ref
    nb = SAMPLE_SEQS_PER_BLOCK
    steps = SAMPLE_STEPS

    vblocks = DV // 128

    def slab(ref, t):
        return ref[t * nb:(t + 1) * nb, :]

    def slab_v(ref, hd, t):
        return jnp.concatenate([ref[hd * vblocks + j, t * nb:(t + 1) * nb, :] for j in range(vblocks)], axis=1)

    def twice(x):
        return jnp.concatenate([x] * vblocks, axis=1)

    m0 = m_ref[...]
    ig = [slab(gi_ref, t) + ib_ref[...] for t in range(steps)]
    lf = [_log_sigmoid(slab(gf_ref, t) + fb_ref[...]) for t in range(steps)]
    b, a, m_t = [], [], []
    run_b = jnp.zeros_like(m0)
    run_max = jnp.full(m0.shape, -jnp.inf, F32)
    for t in range(steps):
        run_b = run_b + lf[t]
        b.append(run_b)
        a.append(ig[t] - run_b)
        run_max = jnp.maximum(run_max, a[t])
        m_t.append(run_b + jnp.maximum(m0, run_max))
    inter = [jnp.exp(b[t] + m0 - m_t[t]) for t in range(steps)]
    m_end = m_t[-1]
    decay = jnp.exp(b[-1] + m0 - m_end)
    w_in = [jnp.exp(a[t] + b[-1] - m_end) for t in range(steps)]
    mn_ref[...] = m_end
    for hd in range(HEADS):
        hcols = slice(hd * DQK, (hd + 1) * DQK)
        for t in range(steps):
            win_scr[hd, t * nb:(t + 1) * nb, :] = w_in[t][:, hcols]
            dec_scr[hd, t * nb:(t + 1) * nb, :] = decay[:, hcols]

    def seq_body(bi, carry):
        rows = pl.ds(bi, steps, stride=nb)
        for hd in range(HEADS):
            c_old = c_ref[bi, hd]
            qb = q_ref[hd, rows, :].astype(BF16)
            cq = _dot_nt(qb, c_old.astype(BF16))
            for j in range(vblocks):
                cq_scr[hd * vblocks + j, rows, :] = cq[:, j * 128:(j + 1) * 128]
            vrows = jnp.concatenate([v_ref[hd * vblocks + j, rows, :] for j in range(vblocks)], axis=1)
            vs = (vrows * twice(win_scr[hd, rows, :])).astype(BF16)
            kb = k_ref[hd, rows, :].astype(BF16)
            dec_row = dec_scr[hd, rows, :][0:1, :]
            cn_ref[bi, hd] = dec_row * c_old + _dot_tn(vs, kb)
        return carry

    lax.fori_loop(0, nb, seq_body, 0, unroll=SAMPLE_SEQ_UNROLL)

    pairs = [(t, s) for t in range(steps) for s in range(t + 1)]
    dfac = [jnp.exp((b[t] - m_t[t]) + a[s]) for (t, s) in pairs]
    enm = [jnp.exp(-m_t[t]) for t in range(steps)]
    for hd in range(HEADS):
        hcols = slice(hd * DQK, (hd + 1) * DQK)
        vcols = slice(hd * DV, (hd + 1) * DV)
        qs = [slab(q_ref.at[hd], t) for t in range(steps)]
        ks = [slab(k_ref.at[hd], t) for t in range(steps)]
        vs = [slab_v(v_ref, hd, t) for t in range(steps)]
        n_old = n_ref[:, hcols]
        n_new = decay[:, hcols] * n_old
        for t in range(steps):
            n_new = n_new + w_in[t][:, hcols] * ks[t]
        nn_ref[:, hcols] = n_new
        prods = jnp.concatenate([qs[t] * ks[s] for (t, s) in pairs] + [qs[t] * n_old for t in range(steps)], axis=0)
        red = jnp.sum(prods, axis=1, keepdims=True)
        wts = red[:len(pairs) * nb] * jnp.concatenate([d[:, hcols] for d in dfac], axis=0)
        for t in range(steps):
            inter_t = inter[t][:, hcols]
            num = twice(inter_t) * slab_v(cq_scr, hd, t)
            den = inter_t * red[(len(pairs) + t) * nb:(len(pairs) + t + 1) * nb]
            for s in range(t + 1):
                p = pairs.index((t, s))
                w_ts = wts[p * nb:(p + 1) * nb]
                num = num + twice(w_ts) * vs[s]
                den = den + w_ts
            hout = num / twice(jnp.maximum(jnp.abs(den), enm[t][:, hcols]))
            hc_ref[t * nb:(t + 1) * nb, vcols] = _rms(hout, normg_ref[:, vcols])


def _mix_s3_kernel(x_ref, outa_ref, outb_ref, hc_ref, gpre_ref, wgate_ref, pa_ref, pb_ref, pc_ref,
                   wout_ref, gpost_ref, o_ref):
    x = x_ref[...]
    h, h_pieces = _prenorm(x, gpre_ref[...])

    def gate(j):
        pieces = h_pieces if j == 0 else [h]
        return _sigmoid(_rows_dot(pieces, _gate_weight(j, D_MODEL, wgate_ref.at[0], col0=OFF_O)))

    o_ref[...] = _merge(x, gate, hc_ref[...], pa_ref, pb_ref, pc_ref, wout_ref, gpost_ref, outa_ref, outb_ref)


def _mix_sample(xs, hist, c_all, c_buf, n_in, m_in, lw, layer):
    rows = xs.shape[0]
    nblk = rows // BLOCK_ROWS
    hist_rows = POOL_HIST * SAMPLE_SEQS_PER_BLOCK
    cs1 = functools.partial(_const_spec, index=layer, ngrid=1)
    full = lambda shape: pl.BlockSpec(shape, lambda i: (0,) * len(shape))
    params = pltpu.CompilerParams(dimension_semantics=("arbitrary",), vmem_limit_bytes=VMEM_LIMIT_BYTES)

    w1 = [lw['mix_norm_pre'], lw['wcat'], lw['wifc'], lw['ln_g'], lw['ln_b'], lw['mixm_s'], lw['mixb_s'],
          lw['pool_w'], lw['pool_scale']]
    s1_specs = [cs1(w.shape) for w in w1]
    s1_specs[1] = pl.BlockSpec((None, D_MODEL, OFF_O), lambda i: (layer, 0, 0), pipeline_mode=pl.Buffered(1))
    s1_shapes = (
        jax.ShapeDtypeStruct((rows, 512), BF16),
        jax.ShapeDtypeStruct((rows, 512), BF16),
        jax.ShapeDtypeStruct((HEADS, rows, DQK), F32),
        jax.ShapeDtypeStruct((HEADS, rows, DQK), F32),
        jax.ShapeDtypeStruct((D_MODEL // 128, rows, 128), F32),
        jax.ShapeDtypeStruct((rows, HEADS * DQK), F32),
        jax.ShapeDtypeStruct((rows, HEADS * DQK), F32),
        jax.ShapeDtypeStruct((rows, 512), F32),
        jax.ShapeDtypeStruct((nblk, hist_rows, 512), F32),
    )
    outa, outb, q, k, v, gi, gf, vn, pool_new = pl.pallas_call(
        _mix_s1_kernel,
        grid=(1,),
        in_specs=[full(xs.shape), full(hist.shape)] + s1_specs,
        out_specs=tuple(full(s.shape) for s in s1_shapes),
        out_shape=s1_shapes,
        scratch_shapes=[pltpu.VMEM((rows, 512), BF16)],
        compiler_params=params,
        name="mix_s1",
    )(xs, hist, *w1)

    nb = SAMPLE_SEQS_PER_BLOCK
    blk = lambda width: pl.BlockSpec((BLOCK_ROWS, width), lambda j: (j, 0))
    seqblk = lambda width: pl.BlockSpec((nb, width), lambda j: (j, 0))
    cblk = pl.BlockSpec((None, nb, HEADS, DV, DQK), lambda j: (layer, j, 0, 0, 0))
    lrow = lambda width: pl.BlockSpec((None, 1, width), lambda j: (layer, 0, 0))
    colblk = lambda n: pl.BlockSpec((n, BLOCK_ROWS, 128), lambda j: (0, j, 0))
    hc, c_buf, n_new, m_new = pl.pallas_call(
        _mix_s2_kernel,
        grid=(nblk,),
        in_specs=[colblk(HEADS), colblk(HEADS), colblk(D_MODEL // 128), blk(512), blk(512),
                  lrow(512), lrow(512), lrow(D_MODEL), cblk, seqblk(512), seqblk(512),
                  pl.BlockSpec(memory_space=pl.ANY)],
        out_specs=(blk(D_MODEL), cblk, seqblk(512), seqblk(512)),
        out_shape=(jax.ShapeDtypeStruct((rows, D_MODEL), F32),
                   jax.ShapeDtypeStruct(c_all.shape, F32),
                   jax.ShapeDtypeStruct(n_in.shape, F32),
                   jax.ShapeDtypeStruct(m_in.shape, F32)),
        input_output_aliases={11: 1},
        scratch_shapes=[pltpu.VMEM((D_MODEL // 128, BLOCK_ROWS, 128), F32),
                        pltpu.VMEM((HEADS, BLOCK_ROWS, 128), F32), pltpu.VMEM((HEADS, BLOCK_ROWS, 128), F32)],
        compiler_params=pltpu.CompilerParams(dimension_semantics=("parallel",),
                                             vmem_limit_bytes=VMEM_LIMIT_BYTES),
        name="mix_s2",
    )(q, k, v, gi, gf, lw['ib'], lw['fb'], lw['norm_g'], c_all, n_in, m_in, c_buf)

    w3 = [lw['mix_norm_pre'], lw['wcat'], lw['proj_a'], lw['proj_b'], lw['proj_c'], lw['w_out'],
          lw['mix_norm_post']]
    s3_specs = [cs1(w.shape) for w in w3]
    s3_specs[1] = pl.BlockSpec((pl.Element(1), pl.Element(D_MODEL), pl.Element(WT_ROWS - OFF_O)),
                               lambda i: (layer, 0, OFF_O), pipeline_mode=pl.Buffered(1))
    y = pl.pallas_call(
        _mix_s3_kernel,
        grid=(1,),
        in_specs=[full(xs.shape), full(outa.shape), full(outb.shape), full(hc.shape)] + s3_specs,
        out_specs=full(xs.shape),
        out_shape=jax.ShapeDtypeStruct(xs.shape, F32),
        compiler_params=params,
        name="mix_s3",
    )(xs, outa, outb, hc, *w3)
    return y, vn, pool_new, c_buf, n_new, m_new


def _to_blocked(x):
    nseq, t, c = x.shape
    nb = SAMPLE_SEQS_PER_BLOCK
    return x.reshape(nseq // nb, nb, t, c).transpose(0, 2, 1, 3).reshape(nseq * t, c)


def _from_blocked(x, t):
    rows, c = x.shape
    nb = SAMPLE_SEQS_PER_BLOCK
    nseq = rows // t
    return x.reshape(nseq // nb, t, nb, c).transpose(0, 2, 1, 3).reshape(nseq, t, c)


def _prep_weights(p):
    w_in = p['w_in']
    row = lambda a: a.reshape(DEPTH, 1, -1).astype(F32)
    wi = w_in[:, :, OFF_G:OFF_G + HEADS]
    wf = w_in[:, :, OFF_G + HEADS:OFF_G + 2 * HEADS]
    wift = jnp.zeros((DEPTH, 32, D_MODEL), F32)
    wift = wift.at[:, 0:HEADS].set(wi.transpose(0, 2, 1)).at[:, 16:16 + HEADS].set(wf.transpose(0, 2, 1))
    wifc = jnp.concatenate([jnp.repeat(wi, DQK, axis=2), jnp.repeat(wf, DQK, axis=2)], axis=2)
    gbias = jnp.zeros((DEPTH, 32, BLOCK_ROWS), F32)
    gbias = gbias.at[:, 0:HEADS].set(jnp.broadcast_to(p['mlstm_i_bias'][:, :, None], (DEPTH, HEADS, BLOCK_ROWS)))
    gbias = gbias.at[:, 16:16 + HEADS].set(
        jnp.broadcast_to(p['mlstm_f_bias'][:, :, None], (DEPTH, HEADS, BLOCK_ROWS)))
    lane_pad = lambda a: jnp.repeat(a.astype(F32), DQK, axis=1).reshape(DEPTH, 1, HEADS * DQK)

    ws = p['gmlp_w_s']
    bs = p['gmlp_b_s']
    mixm_p = jnp.tril(ws)
    mixb_p = jnp.broadcast_to(bs[..., None], ws.shape)
    nb = SAMPLE_SEQS_PER_BLOCK
    tril_s = jnp.tril(ws[:, :, :SAMPLE_STEPS, :SAMPLE_STEPS])
    eye = jnp.eye(nb, dtype=F32)
    mixm_s = jnp.einsum('lgts,bc->lgtbsc', tril_s, eye).reshape(DEPTH, N_GROUPS, BLOCK_ROWS, BLOCK_ROWS)
    mixb_s = jnp.broadcast_to(jnp.repeat(bs[:, :, :SAMPLE_STEPS], nb, axis=2)[..., None], ws.shape)

    return dict(
        ffn1_norm_pre=row(p['ffn1_norm_pre']), ffn1_w_in=p['ffn1_w_in'].astype(BF16),
        ffn1_w_down=p['ffn1_w_down'].astype(BF16), ffn1_norm_post=row(p['ffn1_norm_post']),
        ffn2_norm_pre=row(p['ffn2_norm_pre']), ffn2_w_in=p['ffn2_w_in'].astype(BF16),
        ffn2_w_down=p['ffn2_w_down'].astype(BF16), ffn2_norm_post=row(p['ffn2_norm_post']),
        mix_norm_pre=row(p['mix_norm_pre']), mix_norm_post=row(p['mix_norm_post']),
        w_in_t=jnp.swapaxes(w_in, 1, 2),
        wift=wift.astype(BF16), wifc=wifc.astype(BF16), gbias=gbias,
        ib=lane_pad(p['mlstm_i_bias']), fb=lane_pad(p['mlstm_f_bias']),
        ln_g=row(p['gmlp_ln_g']), ln_b=row(p['gmlp_ln_b']),
        mixm_p=mixm_p.astype(BF16), mixb_p=mixb_p.astype(F32),
        mixm_s=mixm_s.astype(BF16), mixb_s=mixb_s.astype(F32),
        pool_w=p['pool_w'].astype(BF16), pool_scale=row(p['pool_scale']), norm_g=row(p['mlstm_norm_g']),
        proj_a=p['proj_a'].astype(BF16), proj_b=p['proj_b'].astype(BF16), proj_c=p['proj_c'].astype(BF16),
        w_out=p['w_out'].astype(BF16),
    )


def kernel(x_prompt, x_sample, state_pool, state_mlstm_C, state_mlstm_n, state_mlstm_m, ffn1_norm_pre, ffn1_w_in, ffn1_w_down, ffn1_norm_post, mix_norm_pre, w_in, gmlp_ln_g, gmlp_ln_b, gmlp_w_s, gmlp_b_s, pool_w, pool_scale, mlstm_i_bias, mlstm_f_bias, mlstm_norm_g, proj_a, proj_b, proj_c, w_out, mix_norm_post, ffn2_norm_pre, ffn2_w_in, ffn2_w_down, ffn2_norm_post):
    params = dict(ffn1_norm_pre=ffn1_norm_pre, ffn1_w_in=ffn1_w_in, ffn1_w_down=ffn1_w_down,
                  ffn1_norm_post=ffn1_norm_post, mix_norm_pre=mix_norm_pre, w_in=w_in, gmlp_ln_g=gmlp_ln_g,
                  gmlp_ln_b=gmlp_ln_b, gmlp_w_s=gmlp_w_s, gmlp_b_s=gmlp_b_s, pool_w=pool_w,
                  pool_scale=pool_scale, mlstm_i_bias=mlstm_i_bias, mlstm_f_bias=mlstm_f_bias,
                  mlstm_norm_g=mlstm_norm_g, proj_a=proj_a, proj_b=proj_b, proj_c=proj_c, w_out=w_out,
                  mix_norm_post=mix_norm_post, ffn2_norm_pre=ffn2_norm_pre, ffn2_w_in=ffn2_w_in,
                  ffn2_w_down=ffn2_w_down, ffn2_norm_post=ffn2_norm_post)
    lw = _prep_weights(params)
    nbp, seq, _ = x_prompt.shape
    nbs, steps, _ = x_sample.shape

    yp = x_prompt
    ys = _to_blocked(x_sample)
    hist_all = jnp.pad(state_pool, ((0, 0), (0, 0), (1, 0), (0, 0)))
    n_all = state_mlstm_n.reshape(DEPTH, nbs, HEADS * DQK)
    m_all = jnp.repeat(state_mlstm_m, DQK, axis=2)

    pool_p, c_p, n_p, m_p = [], [], [], []
    pool_s, n_s, m_s, v_s = [], [], [], []
    c_sample = None
    for l in range(DEPTH):
        ffn1 = (lw['ffn1_norm_pre'], lw['ffn1_w_in'], lw['ffn1_w_down'], lw['ffn1_norm_post'])
        ffn2 = (lw['ffn2_norm_pre'], lw['ffn2_w_in'], lw['ffn2_w_down'], lw['ffn2_norm_post'])
        yp2d = yp.reshape(nbp * seq, D_MODEL)
        if l == 0:
            yp2d, lw['wcat'] = _ffn(yp2d, *ffn1, l, side='cast', side_arg=lw['w_in_t'])
        else:
            yp2d = _ffn(yp2d, *ffn1, l)
        yp, pb, cf, nf_, mf = _mix_prompt(yp2d.reshape(nbp, seq, D_MODEL), lw, l)
        yp2d = yp.reshape(nbp * seq, D_MODEL)
        if l == 0:
            yp2d, zeros = _ffn(yp2d, *ffn2, l, side='zeros',
                               side_arg=(ZERO_BLOCKS, state_mlstm_C.size // (ZERO_BLOCKS * 128)))
            c_sample = zeros.reshape(state_mlstm_C.shape)
        else:
            yp2d = _ffn(yp2d, *ffn2, l)
        yp = yp2d.reshape(nbp, seq, D_MODEL)
        pool_p.append(pb[:, 1:, :])
        c_p.append(cf)
        n_p.append(nf_[:, :HEADS, :])
        m_p.append(mf[:, :HEADS, 0])
        ys = _ffn(ys, *ffn1, l)
        hist = _to_blocked(hist_all[l]).reshape(nbs // SAMPLE_SEQS_PER_BLOCK,
                                                POOL_HIST * SAMPLE_SEQS_PER_BLOCK, 512)
        ys, vn, pool_new, c_sample, n_new, m_new = _mix_sample(
            ys, hist, state_mlstm_C, c_sample, n_all[l], m_all[l], lw, l)
        ys = _ffn(ys, *ffn2, l)
        pool_s.append(_from_blocked(pool_new.reshape(nbs * POOL_HIST, 512), POOL_HIST)[:, 1:, :])
        n_s.append(n_new.reshape(nbs, HEADS, DQK))
        m_s.append(m_new.reshape(nbs, HEADS, DQK)[:, :, 0])
        v_s.append(_from_blocked(vn, steps))
    return (yp, _from_blocked(ys, steps),
            jnp.stack(pool_p), jnp.stack(c_p), jnp.stack(n_p), jnp.stack(m_p),
            jnp.stack(pool_s), c_sample, jnp.stack(n_s), jnp.stack(m_s), jnp.stack(v_s))
```
